```python
import jax, jax.numpy as jnp
from jax import lax
import numpy as np

D_MODEL = 1024
BATCH = 8
SEQ = 2048
DEPTH = 2
DEC_BATCH = 128
DEC_SEQ = 4
PAST_LEN = 16384
PAGE_SIZE = 128

N_BRANCH = 4
W_BRANCH = D_MODEL // 2
W_A = W_BRANCH
W_B = W_BRANCH
W_C = W_BRANCH
W_D = W_BRANCH
CONV_A_WIDTH = 31
CONV_B_WIDTH = 3
CHUNK = 128
C_GROUPS = 4
C_GROUP_DIM = W_C // C_GROUPS
POOL_WINDOWS = (2, 4, 8, 16)
N_POOL = len(POOL_WINDOWS)
D_GROUP_DIM = W_D // N_POOL
POOL_BUF = max(POOL_WINDOWS) - 1
N_MEM = 256
X_HEADS = 4
X_HEAD_DIM = D_MODEL // X_HEADS
D_FF = 2816
FFN_CONV_WIDTH = 3
EPS = 1e-6
SPLIT_SIZES = (W_A, W_A, W_B, W_B, W_B, 2 * W_C, W_D, N_BRANCH * D_MODEL)
IN_COLS = 2 * W_A + 3 * W_B + 2 * W_C + W_D + N_BRANCH * D_MODEL

kernel_name = "gated_parallel_conv_chunkmlp_pool_decoder_step"


def rmsnorm(x, g):
    xf = x.astype(jnp.float32)
    y = xf * lax.rsqrt(jnp.mean(xf * xf, axis=-1, keepdims=True) + EPS)
    return (y * g.astype(jnp.float32)).astype(x.dtype)


def layernorm(x, g, b):
    xf = x.astype(jnp.float32)
    mu = jnp.mean(xf, axis=-1, keepdims=True)
    var = jnp.mean(jnp.square(xf - mu), axis=-1, keepdims=True)
    return ((xf - mu) * lax.rsqrt(var + EPS) * g.astype(jnp.float32) + b.astype(jnp.float32)).astype(x.dtype)


def causal_dwconv(x, buf, w, b):
    k, c = w.shape
    xc = jnp.concatenate([buf.astype(x.dtype), x], axis=1)
    y = lax.conv_general_dilated(xc, w[:, None, :].astype(x.dtype), window_strides=(1,), padding='VALID',
                                 dimension_numbers=('NWC', 'WIO', 'NWC'), feature_group_count=c)
    return y + b.astype(x.dtype), xc[:, xc.shape[1] - (k - 1):]


def causal_multiscale_pool(x, buf, past):
    t = x.shape[1]
    xc = jnp.concatenate([buf.astype(x.dtype), x], axis=1)
    cs = jnp.cumsum(xc.astype(jnp.float32), axis=1)
    cs = jnp.concatenate([jnp.zeros_like(cs[:, :1]), cs], axis=1)
    end = cs[:, POOL_BUF + 1:]
    pos = past + jnp.arange(t)
    outs = []
    for gi, w in enumerate(POOL_WINDOWS):
        sl = slice(gi * D_GROUP_DIM, (gi + 1) * D_GROUP_DIM)
        start = cs[:, POOL_BUF + 1 - w: POOL_BUF + 1 - w + t, sl]
        cnt = jnp.minimum(w, pos + 1).astype(jnp.float32)[None, :, None]
        outs.append((end[..., sl] - start) / cnt)
    pooled = jnp.concatenate(outs, axis=-1)
    return (pooled - x.astype(jnp.float32)).astype(x.dtype), xc[:, xc.shape[1] - POOL_BUF:]


def chunk_spatial_gate(u, v, w_s, b_s):
    n, t, _ = v.shape
    L = min(t, CHUNK)
    nc = t // L
    mask = jnp.tril(jnp.ones((L, L), dtype=bool))
    ws = jnp.where(mask[None], w_s[:, :L, :L], 0.0).astype(v.dtype)
    vc = v.reshape(n, nc, L, C_GROUPS, C_GROUP_DIM)
    s = jnp.einsum('gts,ncsgd->nctgd', ws, vc) + b_s[:, :L].T.astype(v.dtype)[None, None, :, :, None]
    return u * s.reshape(n, t, W_C)


def decoder_layer(x, mem_k, mem_v, buf_a, buf_b, buf_d, buf_f, past, lw):
    (g_mix, w_in, conv_a_w, conv_a_b, ln_a_g, ln_a_b, w_a_out, conv_b_w, conv_b_b, w_b_out,
     ln_c_g, ln_c_b, w_s, b_s, w_c_out, w_d_grp, scale_d, w_d_out, w_mix_out,
     g_xattn, w_q, w_xo, g_ffn, w_up, ffn_conv_w, ffn_conv_b, w_down) = lw
    n, t, _ = x.shape
    h = rmsnorm(x, g_mix)
    z = h @ w_in
    points = np.cumsum(SPLIT_SIZES)[:-1].tolist()
    a_val, a_gate, gb, gc, hb, uv, xd, gate_pre = jnp.split(z, points, axis=-1)
    a, new_a = causal_dwconv(a_val * jax.nn.sigmoid(a_gate), buf_a, conv_a_w, conv_a_b)
    y_a = jax.nn.silu(layernorm(a, ln_a_g, ln_a_b)) @ w_a_out
    cb, new_b = causal_dwconv(gc * hb, buf_b, conv_b_w, conv_b_b)
    y_b = (gb * cb) @ w_b_out
    u, v = jnp.split(jax.nn.gelu(uv), 2, axis=-1)
    v = layernorm(v, ln_c_g, ln_c_b)
    y_c = chunk_spatial_gate(u, v, w_s, b_s) @ w_c_out
    pd, new_d = causal_multiscale_pool(xd, buf_d, past)
    pd = jnp.einsum('ntgc,gcd->ntgd', pd.reshape(n, t, N_POOL, D_GROUP_DIM), w_d_grp).reshape(n, t, W_D)
    y_d = (pd * scale_d) @ w_d_out
    g_a, g_b, g_c, g_d = jnp.split(jax.nn.sigmoid(gate_pre), N_BRANCH, axis=-1)
    m = g_a * y_a + g_b * y_b + g_c * y_c + g_d * y_d
    x = x + m @ w_mix_out
    q = (rmsnorm(x, g_xattn) @ w_q).reshape(n, t, X_HEADS, X_HEAD_DIM)
    s = jnp.einsum('nthd,nmhd->nhtm', q, mem_k.astype(q.dtype), preferred_element_type=jnp.float32)
    p = jax.nn.softmax(s * (X_HEAD_DIM ** -0.5), axis=-1)
    o = jnp.einsum('nhtm,nmhd->nthd', p.astype(x.dtype), mem_v.astype(x.dtype)).reshape(n, t, D_MODEL)
    x = x + o @ w_xo
    up, new_f = causal_dwconv(rmsnorm(x, g_ffn) @ w_up, buf_f, ffn_conv_w, ffn_conv_b)
    fa, fb = jnp.split(up, 2, axis=-1)
    x = x + (jax.nn.silu(fa) * fb) @ w_down
    return x, new_a, new_b, new_d, new_f, v


def setup_inputs(seed: int = 0) -> dict:
    key = jax.random.key(seed)
    ks = iter(jax.random.split(key, 64))
    f32 = jnp.float32

    def nrm(shape, scale):
        return jax.random.normal(next(ks), shape, f32) * scale

    def gain(shape):
        return 1.0 + 0.01 * jax.random.normal(next(ks), shape, f32)

    L = DEPTH
    return {
        "x_prompt": nrm((BATCH, SEQ, D_MODEL), 1.0),
        "x_sample": nrm((DEC_BATCH, DEC_SEQ, D_MODEL), 1.0),
        "mem_prompt": nrm((BATCH, N_MEM, D_MODEL), 1.0),
        "state_conv_a": nrm((L, DEC_BATCH, CONV_A_WIDTH - 1, W_A), 1.0),
        "state_conv_b": nrm((L, DEC_BATCH, CONV_B_WIDTH - 1, W_B), 1.0),
        "state_pool_d": nrm((L, DEC_BATCH, POOL_BUF, W_D), 1.0),
        "state_ffn_conv": nrm((L, DEC_BATCH, FFN_CONV_WIDTH - 1, 2 * D_FF), 1.0),
        "cache_mem_k": nrm((L, DEC_BATCH, N_MEM, X_HEADS, X_HEAD_DIM), 1.0),
        "cache_mem_v": nrm((L, DEC_BATCH, N_MEM, X_HEADS, X_HEAD_DIM), 1.0),
        "g_mix": gain((L, D_MODEL)),
        "w_in": nrm((L, D_MODEL, IN_COLS), D_MODEL ** -0.5),
        "conv_a_w": nrm((L, CONV_A_WIDTH, W_A), CONV_A_WIDTH ** -0.5),
        "conv_a_b": nrm((L, W_A), 0.01),
        "ln_a_g": gain((L, W_A)),
        "ln_a_b": nrm((L, W_A), 0.01),
        "w_a_out": nrm((L, W_A, D_MODEL), W_A ** -0.5),
        "conv_b_w": nrm((L, CONV_B_WIDTH, W_B), CONV_B_WIDTH ** -0.5),
        "conv_b_b": nrm((L, W_B), 0.01),
        "w_b_out": nrm((L, W_B, D_MODEL), W_B ** -0.5),
        "ln_c_g": gain((L, W_C)),
        "ln_c_b": nrm((L, W_C), 0.01),
        "w_s": nrm((L, C_GROUPS, CHUNK, CHUNK), CHUNK ** -0.5),
        "b_s": 1.0 + nrm((L, C_GROUPS, CHUNK), 0.1),
        "w_c_out": nrm((L, W_C, D_MODEL), W_C ** -0.5),
        "w_d_grp": nrm((L, N_POOL, D_GROUP_DIM, D_GROUP_DIM), D_GROUP_DIM ** -0.5),
        "scale_d": gain((L, W_D)),
        "w_d_out": nrm((L, W_D, D_MODEL), W_D ** -0.5),
        "w_mix_out": nrm((L, D_MODEL, D_MODEL), D_MODEL ** -0.5),
        "g_xattn": gain((L, D_MODEL)),
        "w_q": nrm((L, D_MODEL, D_MODEL), D_MODEL ** -0.5),
        "w_mk": nrm((L, D_MODEL, D_MODEL), D_MODEL ** -0.5),
        "w_mv": nrm((L, D_MODEL, D_MODEL), D_MODEL ** -0.5),
        "w_xo": nrm((L, D_MODEL, D_MODEL), D_MODEL ** -0.5),
        "g_ffn": gain((L, D_MODEL)),
        "w_up": nrm((L, D_MODEL, 2 * D_FF), D_MODEL ** -0.5),
        "ffn_conv_w": nrm((L, FFN_CONV_WIDTH, 2 * D_FF), FFN_CONV_WIDTH ** -0.5),
        "ffn_conv_b": nrm((L, 2 * D_FF), 0.01),
        "w_down": nrm((L, D_FF, D_MODEL), D_FF ** -0.5),
        "g_final": gain((D_MODEL,)),
    }


def reference(x_prompt, x_sample, mem_prompt, state_conv_a, state_conv_b, state_pool_d, state_ffn_conv,
              cache_mem_k, cache_mem_v, g_mix, w_in, conv_a_w, conv_a_b, ln_a_g, ln_a_b, w_a_out,
              conv_b_w, conv_b_b, w_b_out, ln_c_g, ln_c_b, w_s, b_s, w_c_out, w_d_grp, scale_d, w_d_out,
              w_mix_out, g_xattn, w_q, w_mk, w_mv, w_xo, g_ffn, w_up, ffn_conv_w, ffn_conv_b, w_down, g_final):
    nb = x_prompt.shape[0]
    xp, xs = x_prompt, x_sample
    a_p, b_p, d_p, f_p, v_p, mk_p, mv_p = [], [], [], [], [], [], []
    a_s, b_s_new, d_s, f_s, v_s = [], [], [], [], []
    for l in range(DEPTH):
        lw = (g_mix[l], w_in[l], conv_a_w[l], conv_a_b[l], ln_a_g[l], ln_a_b[l], w_a_out[l],
              conv_b_w[l], conv_b_b[l], w_b_out[l], ln_c_g[l], ln_c_b[l], w_s[l], b_s[l], w_c_out[l],
              w_d_grp[l], scale_d[l], w_d_out[l], w_mix_out[l], g_xattn[l], w_q[l], w_xo[l],
              g_ffn[l], w_up[l], ffn_conv_w[l], ffn_conv_b[l], w_down[l])
        mk = (mem_prompt @ w_mk[l]).reshape(nb, N_MEM, X_HEADS, X_HEAD_DIM)
        mv = (mem_prompt @ w_mv[l]).reshape(nb, N_MEM, X_HEADS, X_HEAD_DIM)
        zdt = xp.dtype
        xp, na, nbf, nd, nf, vv = decoder_layer(
            xp, mk, mv,
            jnp.zeros((nb, CONV_A_WIDTH - 1, W_A), zdt), jnp.zeros((nb, CONV_B_WIDTH - 1, W_B), zdt),
            jnp.zeros((nb, POOL_BUF, W_D), zdt), jnp.zeros((nb, FFN_CONV_WIDTH - 1, 2 * D_FF), zdt),
            0, lw)
        a_p.append(na); b_p.append(nbf); d_p.append(nd); f_p.append(nf)
        v_p.append(vv[:, vv.shape[1] - CHUNK:]); mk_p.append(mk); mv_p.append(mv)
        xs, na, nbf, nd, nf, vv = decoder_layer(
            xs, cache_mem_k[l], cache_mem_v[l], state_conv_a[l], state_conv_b[l], state_pool_d[l],
            state_ffn_conv[l], PAST_LEN, lw)
        a_s.append(na); b_s_new.append(nbf); d_s.append(nd); f_s.append(nf); v_s.append(vv)
    y_prompt = rmsnorm(xp, g_final)
    y_sample = rmsnorm(xs, g_final)
    return (y_prompt, y_sample,
            jnp.stack(a_p), jnp.stack(b_p), jnp.stack(d_p), jnp.stack(f_p), jnp.stack(v_p),
            jnp.stack(mk_p), jnp.stack(mv_p),
            jnp.stack(a_s), jnp.stack(b_s_new), jnp.stack(d_s), jnp.stack(f_s), jnp.stack(v_s))
```

```python
import functools
from typing import NamedTuple

import jax
import jax.numpy as jnp
from jax import lax
from jax.experimental import pallas as pl
from jax.experimental.pallas import tpu as pltpu

F32 = jnp.float32
BF16 = jnp.bfloat16

EPS = 1e-6
LANES = 128
CHUNK = 128
POOL_WINDOWS = (2, 4, 8, 16)
N_GROUPS = 4
X_HEADS = 4
PAST_LEN = 16384
VMEM_LIMIT_BYTES = 56 * 1024 * 1024


class Tile(NamedTuple):
    nb: int
    tt: int
    nblk: int
    ntile: int
    past: int
    seq_len: int


def _rms(x, g):
    ms = jnp.mean(x * x, axis=-1, keepdims=True)
    return x * lax.rsqrt(ms + EPS) * g


def _ln(x, g, b):
    mu = jnp.mean(x, axis=-1, keepdims=True)
    xc = x - mu
    var = jnp.mean(xc * xc, axis=-1, keepdims=True)
    return xc * lax.rsqrt(var + EPS) * g + b


def _dot(a, b):
    return jnp.dot(a, b, preferred_element_type=F32)


def _rows(ref, b, tt):
    if len(ref.shape) == 3:
        return ref[b]
    return ref[b * tt:(b + 1) * tt, :]


def _set_rows(ref, b, tt, val):
    if len(ref.shape) == 3:
        ref[b] = val
    else:
        ref[b * tt:(b + 1) * tt, :] = val


def _to_tm(tm, b, val, t):
    for c in range(val.shape[1] // LANES):
        tm[c, pl.ds(b, t.tt, stride=t.nb), :] = val[:, c * LANES:(c + 1) * LANES]


def _from_tm(tm, b, t):
    return jnp.concatenate([tm[c, pl.ds(b, t.tt, stride=t.nb), :] for c in range(tm.shape[0])], axis=1)


def _tm_all(tm):
    return jnp.concatenate([tm[c] for c in range(tm.shape[0])], axis=1)


def _set_tm(tm, val):
    for c in range(tm.shape[0]):
        tm[c] = val[:, c * LANES:(c + 1) * LANES]


def _mix_kernel(t, x_ref, sa_ref, sb_ref, sd_ref,
                gmix_ref, win_ref, caw_ref, cab_ref, lag_ref, lab_ref, wao_ref,
                cbw_ref, cbb_ref, wbo_ref, lcg_ref, lcb_ref, ws_ref, bs_ref, wco_ref,
                wdg_ref, scd_ref, wdo_ref, wmix_ref, gx_ref, wq_ref,
                xo_ref, q_ref, ta_ref, tb_ref, td_ref, tv_ref,
                tm, tm2, h_s, ext_a, hist_b, hist_d, act_s, m_s, *cbufs):
    nb, tt = t.nb, t.tt
    m = nb * tt
    d = h_s.shape[1]
    w = act_s.shape[1]
    ka = caw_ref.shape[0]
    kb = cbw_ref.shape[0]
    hpa, hpb, hpd = (ka - 1) * nb, (kb - 1) * nb, hist_d.shape[0]
    j = pl.program_id(1)
    last = t.ntile - 1

    @pl.when(j == 0)
    def _():
        ext_a[0:hpa, :] = sa_ref[...]
        hist_b[...] = sb_ref[...]
        hist_d[...] = sd_ref[...]

    for b in range(nb):
        _to_tm(tm, b, _rms(_rows(x_ref, b, tt), gmix_ref[...]), t)
    h_s[...] = _tm_all(tm).astype(BF16)

    def seg(lo, hi):
        return _dot(h_s[...], win_ref[:, lo:hi])

    gate0 = 8 * w

    def gate(i):
        return jax.nn.sigmoid(seg(gate0 + i * d, gate0 + (i + 1) * d))

    za = seg(0, 2 * w)
    ext_a[hpa:hpa + m, :] = za[:, :w] * jax.nn.sigmoid(za[:, w:])
    rc = 32

    def conv_chunk(i, carry):
        r0 = pl.multiple_of(i * rc, rc)
        acc = jnp.broadcast_to(cab_ref[...], (rc, w))
        for k in range(ka):
            off = pl.multiple_of(r0 + k * nb, 8)
            acc = acc + caw_ref[k:k + 1, :] * ext_a[pl.ds(off, rc), :]
        y = _ln(acc, lag_ref[...], lab_ref[...])
        act_s[pl.ds(r0, rc), :] = (y * jax.nn.sigmoid(y)).astype(BF16)
        return carry

    lax.fori_loop(0, m // rc, conv_chunk, 0)
    m_s[...] = gate(0) * _dot(act_s[...], wao_ref[...])

    @pl.when(j == last)
    def _():
        ra = ta_ref.shape[0]
        ta_ref[...] = ext_a[hpa + m - ra:hpa + m, :]

    if t.ntile > 1:
        ext_a[0:hpa, :] = ext_a[m:m + hpa, :]

    zb = seg(2 * w, 5 * w)
    ext_b = jnp.concatenate([hist_b[...], zb[:, w:2 * w] * zb[:, 2 * w:3 * w]], axis=0)
    cb = cbb_ref[...] + cbw_ref[0:1, :] * ext_b[0:m]
    for k in range(1, kb):
        cb = cb + cbw_ref[k:k + 1, :] * ext_b[k * nb:k * nb + m]
    hist_b[...] = ext_b[m:m + hpb]
    m_s[...] += gate(1) * _dot((zb[:, :w] * cb).astype(BF16), wbo_ref[...])

    @pl.when(j == last)
    def _():
        tb_ref[...] = hist_b[...]

    zc = jax.nn.gelu(seg(5 * w, 7 * w))
    u = zc[:, :w]
    v = _ln(zc[:, w:], lcg_ref[...], lcb_ref[...])
    gw = w // N_GROUPS
    clen = min(t.seq_len, CHUNK)
    if clen == tt and tt < 8:
        s_rows = []
        for ti in range(tt):
            s_t = jnp.broadcast_to(bs_ref[ti:ti + 1, :], (nb, w))
            for si in range(ti + 1):
                s_t = s_t + ws_ref[ti, si:si + 1, :] * v[si * nb:(si + 1) * nb]
            s_rows.append(s_t)
        s_all = jnp.concatenate(s_rows, axis=0)

        @pl.when(j == last)
        def _():
            tv_ref[...] = v
    else:
        vbuf, sbuf = cbufs
        subs = clen // tt
        sub = j % subs
        row0 = pl.multiple_of(sub * m, m)
        for g in range(N_GROUPS):
            vbuf[g, pl.ds(row0, m), :] = v[:, g * gw:(g + 1) * gw]
        if subs > 1:
            @pl.when(sub == 0)
            def _():
                for g in range(N_GROUPS):
                    vbuf[g, m:subs * m, :] = jnp.zeros(((subs - 1) * m, gw), F32)
        t0 = pl.multiple_of(sub * tt, tt)
        ri = t0 + lax.broadcasted_iota(jnp.int32, (tt, clen), 0)
        ci = lax.broadcasted_iota(jnp.int32, (tt, clen), 1)
        bias = bs_ref[pl.ds(t0, tt), :]
        for g in range(N_GROUPS):
            vg = jnp.concatenate([vbuf[g, pl.ds(b, clen, stride=nb), :] for b in range(nb)], axis=1)
            wsg = jnp.where(ri >= ci, ws_ref[g, pl.ds(t0, tt), :], 0.0).astype(BF16)
            sg = _dot(wsg, vg.astype(BF16)) + bias[:, g:g + 1]
            for b in range(nb):
                sbuf[g, pl.ds(b, tt, stride=nb), :] = sg[:, b * gw:(b + 1) * gw]
        s_all = _tm_all(sbuf)

        @pl.when(j == last)
        def _():
            tv_ref[...] = _tm_all(vbuf)
    m_s[...] += gate(2) * _dot((u * s_all).astype(BF16), wco_ref[...])

    xd = seg(7 * w, 8 * w)
    ext_d = jnp.concatenate([hist_d[...], xd], axis=0)
    pos = t.past + j * tt + lax.broadcasted_iota(jnp.int32, (m, gw), 0) // nb
    pgs = []
    for gi, win in enumerate(POOL_WINDOWS):
        cs = slice(gi * gw, (gi + 1) * gw)
        acc = xd[:, cs]
        for lag in range(1, win):
            acc = acc + ext_d[hpd - lag * nb:hpd - lag * nb + m, cs]
        cnt = jnp.minimum(win, pos + 1).astype(F32)
        pgs.append(_dot((acc / cnt - xd[:, cs]).astype(BF16), wdg_ref[gi]))
    hist_d[...] = ext_d[m:m + hpd]
    pd = jnp.concatenate(pgs, axis=1) * scd_ref[...]
    m_s[...] += gate(3) * _dot(pd.astype(BF16), wdo_ref[...])

    @pl.when(j == last)
    def _():
        rd = td_ref.shape[0]
        td_ref[...] = ext_d[hpd + m - rd:hpd + m]

    _set_tm(tm, _dot(m_s[...].astype(BF16), wmix_ref[...]))
    for b in range(nb):
        x1 = _rows(x_ref, b, tt) + _from_tm(tm, b, t)
        _set_rows(xo_ref, b, tt, x1)
        _to_tm(tm2, b, _rms(x1, gx_ref[...]), t)
    scale = float(d // X_HEADS) ** -0.5
    _set_tm(tm, _dot(_tm_all(tm2).astype(BF16), wq_ref[...]) * scale)
    for b in range(nb):
        _set_rows(q_ref, b, tt, _from_tm(tm, b, t).astype(q_ref.dtype))


def _const(shape, layer):
    nz = len(shape)
    return pl.BlockSpec((None,) + tuple(shape), lambda i, j: (layer,) + (0,) * nz,
                        pipeline_mode=pl.Buffered(1))


def _mix_call(t, layer, x, sa, sb, sd, wts, q_dtype):
    (gmix, win, caw, cab, lag, lab, wao, cbw, cbb, wbo, lcg, lcb, ws, bs, wco,
     wdg, scd, wdo, wmix, gx, wq) = wts
    nb, tt = t.nb, t.tt
    m = nb * tt
    d = win.shape[1]
    w = wao.shape[1]
    ka, kb = caw.shape[1], cbw.shape[1]
    hpa, hpb, hpd = sa.shape[1], sb.shape[1], sd.shape[1]
    ra, rd = min(hpa, m), min(hpd, m)
    clen = min(t.seq_len, CHUNK)
    short_chunk = clen == tt and tt < 8
    rv = m if short_chunk else clen * nb
    if x.ndim == 3:
        x_spec = pl.BlockSpec((nb, tt, d), lambda i, j: (i, j, 0))
        x_shape = x.shape
    else:
        x_spec = pl.BlockSpec((m, d), lambda i, j: (i * t.ntile + j, 0))
        x_shape = x.shape

    def state_spec(rows):
        return pl.BlockSpec((None, rows, w), lambda i, j: (i, 0, 0))

    in_specs = [x_spec, state_spec(hpa), state_spec(hpb), state_spec(hpd)]
    in_specs += [_const(a.shape[1:], layer) for a in wts]
    out_shape = [jax.ShapeDtypeStruct(x_shape, F32), jax.ShapeDtypeStruct(x_shape, q_dtype),
                 jax.ShapeDtypeStruct((t.nblk, ra, w), F32), jax.ShapeDtypeStruct((t.nblk, hpb, w), F32),
                 jax.ShapeDtypeStruct((t.nblk, rd, w), F32), jax.ShapeDtypeStruct((t.nblk, rv, w), F32)]
    out_specs = [x_spec, x_spec, state_spec(ra), state_spec(hpb), state_spec(rd), state_spec(rv)]
    scratch = [pltpu.VMEM((d // LANES, m, LANES), F32), pltpu.VMEM((d // LANES, m, LANES), F32),
               pltpu.VMEM((m, d), BF16), pltpu.VMEM((hpa + m, w), F32),
               pltpu.VMEM((hpb, w), F32), pltpu.VMEM((hpd, w), F32),
               pltpu.VMEM((m, w), BF16), pltpu.VMEM((m, d), F32)]
    if not short_chunk:
        scratch += [pltpu.VMEM((N_GROUPS, clen * nb, w // N_GROUPS), F32),
                    pltpu.VMEM((N_GROUPS, m, w // N_GROUPS), F32)]
    return pl.pallas_call(
        functools.partial(_mix_kernel, t),
        grid=(t.nblk, t.ntile),
        in_specs=in_specs, out_specs=out_specs, out_shape=out_shape,
        scratch_shapes=scratch,
        compiler_params=pltpu.CompilerParams(
            dimension_semantics=("arbitrary", "arbitrary"), vmem_limit_bytes=VMEM_LIMIT_BYTES),
        name=f"mix_l{layer}_nb{nb}",
    )(x, sa, sb, sd, *wts)


def _att_kernel(q_ref, k_ref, v_ref, o_ref):
    sb = q_ref.shape[0]
    d = q_ref.shape[2]
    hd = d // X_HEADS

    def per_seq(b, carry):
        q = q_ref[b].astype(BF16)
        outs = []
        for h in range(X_HEADS):
            cs = slice(h * hd, (h + 1) * hd)
            kh = k_ref[b, :, cs].astype(BF16)
            vh = v_ref[b, :, cs].astype(BF16)
            s = lax.dot_general(q[:, cs], kh, (((1,), (1,)), ((), ())), preferred_element_type=F32)
            e = jnp.exp(s - jnp.max(s, axis=-1, keepdims=True))
            p = e / jnp.sum(e, axis=-1, keepdims=True)
            outs.append(_dot(p.astype(BF16), vh))
        o_ref[b] = jnp.concatenate(outs, axis=1).astype(o_ref.dtype)
        return carry

    if sb == 1:
        per_seq(0, 0)
    else:
        lax.fori_loop(0, sb, per_seq, 0)


def _att_call(q, k, v, layer, sb, tq, name):
    n, tlen, d = q.shape
    nm = k.shape[2]
    q_spec = pl.BlockSpec((sb, tq, d), lambda i, j: (i, j, 0))
    kv_spec = pl.BlockSpec((None, sb, nm, d), lambda i, j: (layer, i, 0, 0))
    return pl.pallas_call(
        _att_kernel,
        grid=(n // sb, tlen // tq),
        in_specs=[q_spec, kv_spec, kv_spec], out_specs=q_spec,
        out_shape=jax.ShapeDtypeStruct(q.shape, q.dtype),
        compiler_params=pltpu.CompilerParams(
            dimension_semantics=("arbitrary", "arbitrary"), vmem_limit_bytes=VMEM_LIMIT_BYTES),
        name=name,
    )(q, k, v)


def _ffn_kernel(t, final, x_ref, o_ref, sf_ref, wxo_ref, gffn_ref, wup_ref, fcw_ref, fcb_ref,
                wdown_ref, gfin_ref, xo_ref, tf_ref, tm, h_s, hist_f, act_s):
    nb, tt = t.nb, t.tt
    m = nb * tt
    d = h_s.shape[1]
    dff = act_s.shape[1]
    kf = fcw_ref.shape[0]
    hp = (kf - 1) * nb
    j = pl.program_id(1)

    @pl.when(j == 0)
    def _():
        hist_f[...] = sf_ref[...]

    o2 = o_ref[...]
    if len(o2.shape) == 3:
        o2 = o2.reshape(m, d)
    ao = _dot(o2.astype(BF16), wxo_ref[...])
    for b in range(nb):
        x2 = _rows(x_ref, b, tt) + ao[b * tt:(b + 1) * tt]
        _set_rows(xo_ref, b, tt, x2)
        _to_tm(tm, b, _rms(x2, gffn_ref[...]), t)
    h_s[...] = _tm_all(tm).astype(BF16)

    cw = 256
    for c in range(dff // cw):
        parts = []
        for col0 in (c * cw, dff + c * cw):
            cs = slice(col0, col0 + cw)
            up = _dot(h_s[...], wup_ref[:, cs])
            ext = jnp.concatenate([hist_f[:, cs], up], axis=0)
            cv = fcb_ref[:, cs] + fcw_ref[0:1, cs] * ext[0:m]
            for k in range(1, kf):
                cv = cv + fcw_ref[k:k + 1, cs] * ext[k * nb:k * nb + m]
            hist_f[:, cs] = ext[m:m + hp]
            parts.append(cv)
        act_s[:, c * cw:(c + 1) * cw] = (parts[0] * jax.nn.sigmoid(parts[0]) * parts[1]).astype(BF16)

    @pl.when(j == t.ntile - 1)
    def _():
        tf_ref[...] = hist_f[...]

    _set_tm(tm, _dot(act_s[...], wdown_ref[...]))
    for b in range(nb):
        x3 = _rows(xo_ref, b, tt) + _from_tm(tm, b, t)
        if final:
            x3 = _rms(x3, gfin_ref[...])
        _set_rows(xo_ref, b, tt, x3)


def _ffn_call(t, layer, final, x, o, sf, wxo, gffn, wup, fcw, fcb, wdown, gfin):
    nb, tt = t.nb, t.tt
    m = nb * tt
    d = wxo.shape[1]
    dff = wdown.shape[1]
    hp = sf.shape[1]
    if x.ndim == 3:
        x_spec = pl.BlockSpec((nb, tt, d), lambda i, j: (i, j, 0))
    else:
        x_spec = pl.BlockSpec((m, d), lambda i, j: (i * t.ntile + j, 0))
    sf_spec = pl.BlockSpec((None, hp, 2 * dff), lambda i, j: (i, 0, 0))
    wts = (wxo, gffn, wup, fcw, fcb, wdown)
    in_specs = [x_spec, x_spec, sf_spec] + [_const(a.shape[1:], layer) for a in wts]
    in_specs.append(pl.BlockSpec(gfin.shape, lambda i, j: (0, 0), pipeline_mode=pl.Buffered(1)))
    return pl.pallas_call(
        functools.partial(_ffn_kernel, t, final),
        grid=(t.nblk, t.ntile),
        in_specs=in_specs, out_specs=[x_spec, sf_spec],
        out_shape=[jax.ShapeDtypeStruct(x.shape, F32), jax.ShapeDtypeStruct((t.nblk, hp, 2 * dff), F32)],
        scratch_shapes=[pltpu.VMEM((d // LANES, m, LANES), F32), pltpu.VMEM((m, d), BF16),
                        pltpu.VMEM((hp, 2 * dff), F32), pltpu.VMEM((m, dff), BF16)],
        compiler_params=pltpu.CompilerParams(
            dimension_semantics=("arbitrary", "arbitrary"), vmem_limit_bytes=VMEM_LIMIT_BYTES),
        name=f"ffn_l{layer}_nb{nb}",
    )(x, o, sf, *wts, gfin)


def _kv_kernel(x_ref, wk_ref, wv_ref, k_ref, v_ref):
    x = x_ref[...].astype(BF16)
    k_ref[...] = _dot(x, wk_ref[...])
    v_ref[...] = _dot(x, wv_ref[...])


def _kv_call(mem2d, wk, wv, layer):
    rows, d = mem2d.shape
    bm = 512
    r_spec = pl.BlockSpec((bm, d), lambda i: (i, 0))
    w_spec = pl.BlockSpec((None, d, d), lambda i: (layer, 0, 0), pipeline_mode=pl.Buffered(1))
    return pl.pallas_call(
        _kv_kernel, grid=(rows // bm,),
        in_specs=[r_spec, w_spec, w_spec], out_specs=[r_spec, r_spec],
        out_shape=[jax.ShapeDtypeStruct((rows, d), F32)] * 2,
        compiler_params=pltpu.CompilerParams(dimension_semantics=("arbitrary",)),
        name=f"kv_l{layer}",
    )(mem2d, wk, wv)


def _state_to_tm(state, t):
    n, h, c = state.shape
    return state.reshape(t.nblk, t.nb, h, c).transpose(0, 2, 1, 3).reshape(t.nblk, h * t.nb, c)


def _tail_to_state(tail, old, t, h):
    nblk, r, c = tail.shape
    k = r // t.nb
    new = tail.reshape(nblk, k, t.nb, c).transpose(0, 2, 1, 3).reshape(nblk * t.nb, k, c)
    if k < h:
        new = jnp.concatenate([old[:, k:], new], axis=1)
    return new


def kernel(x_prompt, x_sample, mem_prompt, state_conv_a, state_conv_b, state_pool_d, state_ffn_conv,
           cache_mem_k, cache_mem_v, g_mix, w_in, conv_a_w, conv_a_b, ln_a_g, ln_a_b, w_a_out,
           conv_b_w, conv_b_b, w_b_out, ln_c_g, ln_c_b, w_s, b_s, w_c_out, w_d_grp, scale_d, w_d_out,
           w_mix_out, g_xattn, w_q, w_mk, w_mv, w_xo, g_ffn, w_up, ffn_conv_w, ffn_conv_b, w_down, g_final):
    depth = w_in.shape[0]
    nbp, seq, d = x_prompt.shape
    nbs, dseq, _ = x_sample.shape
    w = w_a_out.shape[1]
    nm = mem_prompt.shape[1]
    ha, hb, hd, hf = conv_a_w.shape[1] - 1, conv_b_w.shape[1] - 1, max(POOL_WINDOWS) - 1, ffn_conv_w.shape[1] - 1

    tp = Tile(nb=nbp, tt=64, nblk=1, ntile=seq // 64, past=0, seq_len=seq)
    ts = Tile(nb=32, tt=dseq, nblk=nbs // 32, ntile=1, past=PAST_LEN, seq_len=dseq)

    row = lambda a: a.reshape(depth, 1, a.shape[-1])
    bf = lambda a: a.astype(BF16)
    common = dict(
        gmix=row(g_mix), win=bf(w_in), caw=conv_a_w, cab=row(conv_a_b), lag=row(ln_a_g), lab=row(ln_a_b),
        wao=bf(w_a_out), cbw=conv_b_w, cbb=row(conv_b_b), wbo=bf(w_b_out), lcg=row(ln_c_g), lcb=row(ln_c_b),
        wco=bf(w_c_out), wdg=bf(w_d_grp), scd=row(scale_d), wdo=bf(w_d_out), wmix=bf(w_mix_out),
        gx=row(g_xattn), wq=bf(w_q))
    order = ("gmix", "win", "caw", "cab", "lag", "lab", "wao", "cbw", "cbb", "wbo", "lcg", "lcb",
             "ws", "bs", "wco", "wdg", "scd", "wdo", "wmix", "gx", "wq")
    wts_p = dict(common, ws=w_s, bs=b_s.transpose(0, 2, 1))
    gw = w // N_GROUPS
    ws_x = jnp.repeat(w_s[:, :, :dseq, :dseq].transpose(0, 2, 3, 1), gw, axis=-1)
    bs_x = jnp.repeat(b_s[:, :, :dseq].transpose(0, 2, 1), gw, axis=-1)
    wts_s = dict(common, ws=ws_x, bs=bs_x)
    wts_p = tuple(wts_p[k] for k in order)
    wts_s = tuple(wts_s[k] for k in order)

    wxo_b, wup_b, wdown_b, wmk_b, wmv_b = bf(w_xo), bf(w_up), bf(w_down), bf(w_mk), bf(w_mv)
    gffn_r, fcb_r, gfin_r = row(g_ffn), row(ffn_conv_b), g_final.reshape(1, d)

    xp = x_prompt
    xs = x_sample.reshape(nbs * dseq, d)
    mem2d = mem_prompt.reshape(nbp * nm, d)
    ck = cache_mem_k.reshape(depth, nbs, nm, d)
    cv = cache_mem_v.reshape(depth, nbs, nm, d)
    zeros = lambda h, c: jnp.zeros((1, h * nbp, c), F32)

    outs = {k: [] for k in ("a_p", "b_p", "d_p", "f_p", "v_p", "mk", "mv", "a_s", "b_s", "d_s", "f_s", "v_s")}
    for l in range(depth):
        final = l == depth - 1
        mk, mv = _kv_call(mem2d, wmk_b, wmv_b, l)
        xp, q, ta, tb, td, tv = _mix_call(tp, l, xp, zeros(ha, w), zeros(hb, w), zeros(hd, w), wts_p, BF16)
        o = _att_call(q, mk.reshape(1, nbp, nm, d), mv.reshape(1, nbp, nm, d), 0, 1, min(seq, 512), f"att_p_l{l}")
        xp, tf = _ffn_call(tp, l, final, xp, o, zeros(hf, ffn_conv_w.shape[2]), wxo_b, gffn_r, wup_b,
                           ffn_conv_w, fcb_r, wdown_b, gfin_r)
        outs["a_p"].append(_tail_to_state(ta, None, tp, ha))
        outs["b_p"].append(_tail_to_state(tb, None, tp, hb))
        outs["d_p"].append(_tail_to_state(td, None, tp, hd))
        outs["f_p"].append(_tail_to_state(tf, None, tp, hf))
        outs["v_p"].append(_tail_to_state(tv, None, tp, CHUNK))
        outs["mk"].append(mk.reshape(nbp, nm, X_HEADS, d // X_HEADS))
        outs["mv"].append(mv.reshape(nbp, nm, X_HEADS, d // X_HEADS))
        sa, sb_, sd, sf = state_conv_a[l], state_conv_b[l], state_pool_d[l], state_ffn_conv[l]
        xs, q, ta, tb, td, tv = _mix_call(ts, l, xs, _state_to_tm(sa, ts), _state_to_tm(sb_, ts),
                                          _state_to_tm(sd, ts), wts_s, F32)
        o = _att_call(q.reshape(nbs, dseq, d), ck, cv, l, 8, dseq, f"att_s_l{l}")
        xs, tf = _ffn_call(ts, l, final, xs, o.reshape(nbs * dseq, d), _state_to_tm(sf, ts), wxo_b, gffn_r,
                           wup_b, ffn_conv_w, fcb_r, wdown_b, gfin_r)
        outs["a_s"].append(_tail_to_state(ta, sa, ts, ha))
        outs["b_s"].append(_tail_to_state(tb, sb_, ts, hb))
        outs["d_s"].append(_tail_to_state(td, sd, ts, hd))
        outs["f_s"].append(_tail_to_state(tf, sf, ts, hf))
        outs["v_s"].append(_tail_to_state(tv, None, ts, dseq))

    st = lambda k: jnp.stack(outs[k])
    return (xp, xs.reshape(nbs, dseq, d),
            st("a_p"), st("b_p"), st("d_p"), st("f_p"), st("v_p"), st("mk"), st("mv"),
            st("a_s"), st("b_s"), st("d_s"), st("f_s"), st("v_s"))
```

```python
import functools
from typing import NamedTuple

import jax
import jax.numpy as jnp
from jax import lax
from jax.experimental import pallas as pl
from jax.experimental.pallas import tpu as pltpu

F32 = jnp.float32
BF16 = jnp.bfloat16

EPS = 1e-6
LANES = 128
CHUNK = 128
POOL_WINDOWS = (2, 4, 8, 16)
N_GROUPS = 4
X_HEADS = 4
PAST_LEN = 16384
VMEM_LIMIT_BYTES = 56 * 1024 * 1024


class Tile(NamedTuple):
    nb: int
    tt: int
    nblk: int
    ntile: int
    past: int
    seq_len: int


def _rms(x, g):
    ms = jnp.mean(x * x, axis=-1, keepdims=True)
    return x * lax.rsqrt(ms + EPS) * g


def _ln(x, g, b):
    mu = jnp.mean(x, axis=-1, keepdims=True)
    xc = x - mu
    var = jnp.mean(xc * xc, axis=-1, keepdims=True)
    return xc * lax.rsqrt(var + EPS) * g + b


def _dot(a, b):
    return jnp.dot(a, b, preferred_element_type=F32)


def _rows(ref, b, tt):
    if len(ref.shape) == 3:
        return ref[b]
    return ref[b * tt:(b + 1) * tt, :]


def _set_rows(ref, b, tt, val):
    if len(ref.shape) == 3:
        ref[b] = val
    else:
        ref[b * tt:(b + 1) * tt, :] = val


def _to_tm(tm, b, val, t):
    for c in range(val.shape[1] // LANES):
        tm[c, pl.ds(b, t.tt, stride=t.nb), :] = val[:, c * LANES:(c + 1) * LANES]


def _from_tm(tm, b, t):
    return jnp.concatenate([tm[c, pl.ds(b, t.tt, stride=t.nb), :] for c in range(tm.shape[0])], axis=1)


def _tm_all(tm):
    return jnp.concatenate([tm[c] for c in range(tm.shape[0])], axis=1)


def _set_tm(tm, val):
    for c in range(tm.shape[0]):
        tm[c] = val[:, c * LANES:(c + 1) * LANES]


def _mix_kernel(t, x_ref, sa_ref, sb_ref, sd_ref,
                gmix_ref, win_ref, caw_ref, cab_ref, lag_ref, lab_ref, wao_ref,
                cbw_ref, cbb_ref, wbo_ref, lcg_ref, lcb_ref, ws_ref, bs_ref, wco_ref,
                wdg_ref, scd_ref, wdo_ref, wmix_ref, gx_ref, wq_ref,
                xo_ref, q_ref, ta_ref, tb_ref, td_ref, tv_ref,
                tm, tm2, h_s, ext_a, hist_b, hist_d, act_s, m_s, *cbufs):
    nb, tt = t.nb, t.tt
    m = nb * tt
    d = h_s.shape[1]
    w = act_s.shape[1]
    ka = caw_ref.shape[0]
    kb = cbw_ref.shape[0]
    hpa, hpb, hpd = (ka - 1) * nb, (kb - 1) * nb, hist_d.shape[0]
    j = pl.program_id(1)
    last = t.ntile - 1

    @pl.when(j == 0)
    def _():
        ext_a[0:hpa, :] = sa_ref[...]
        hist_b[...] = sb_ref[...]
        hist_d[...] = sd_ref[...]

    for b in range(nb):
        _to_tm(tm, b, _rms(_rows(x_ref, b, tt), gmix_ref[...]), t)
    h_s[...] = _tm_all(tm).astype(BF16)

    def seg(lo, hi):
        return _dot(h_s[...], win_ref[:, lo:hi])

    gate0 = 8 * w

    def gate(i):
        return jax.nn.sigmoid(seg(gate0 + i * d, gate0 + (i + 1) * d))

    za = seg(0, 2 * w)
    ext_a[hpa:hpa + m, :] = za[:, :w] * jax.nn.sigmoid(za[:, w:])
    rc = 32

    def conv_chunk(i, carry):
        r0 = pl.multiple_of(i * rc, rc)
        acc = jnp.broadcast_to(cab_ref[...], (rc, w))
        for k in range(ka):
            off = pl.multiple_of(r0 + k * nb, 8)
            acc = acc + caw_ref[k:k + 1, :] * ext_a[pl.ds(off, rc), :]
        y = _ln(acc, lag_ref[...], lab_ref[...])
        act_s[pl.ds(r0, rc), :] = (y * jax.nn.sigmoid(y)).astype(BF16)
        return carry

    lax.fori_loop(0, m // rc, conv_chunk, 0)
    m_s[...] = gate(0) * _dot(act_s[...], wao_ref[...])

    @pl.when(j == last)
    def _():
        ra = ta_ref.shape[0]
        ta_ref[...] = ext_a[hpa + m - ra:hpa + m, :]

    if t.ntile > 1:
        ext_a[0:hpa, :] = ext_a[m:m + hpa, :]

    zb = seg(2 * w, 5 * w)
    ext_b = jnp.concatenate([hist_b[...], zb[:, w:2 * w] * zb[:, 2 * w:3 * w]], axis=0)
    cb = cbb_ref[...] + cbw_ref[0:1, :] * ext_b[0:m]
    for k in range(1, kb):
        cb = cb + cbw_ref[k:k + 1, :] * ext_b[k * nb:k * nb + m]
    hist_b[...] = ext_b[m:m + hpb]
    m_s[...] += gate(1) * _dot((zb[:, :w] * cb).astype(BF16), wbo_ref[...])

    @pl.when(j == last)
    def _():
        tb_ref[...] = hist_b[...]

    zc = jax.nn.gelu(seg(5 * w, 7 * w))
    u = zc[:, :w]
    v = _ln(zc[:, w:], lcg_ref[...], lcb_ref[...])
    gw = w // N_GROUPS
    clen = min(t.seq_len, CHUNK)
    if clen == tt and tt < 8:
        s_rows = []
        for ti in range(tt):
            s_t = jnp.broadcast_to(bs_ref[ti:ti + 1, :], (nb, w))
            for si in range(ti + 1):
                s_t = s_t + ws_ref[ti, si:si + 1, :] * v[si * nb:(si + 1) * nb]
            s_rows.append(s_t)
        s_all = jnp.concatenate(s_rows, axis=0)

        @pl.when(j == last)
        def _():
            tv_ref[...] = v
    else:
        vbuf, sbuf = cbufs
        subs = clen // tt
        sub = j % subs
        row0 = pl.multiple_of(sub * m, m)
        for g in range(N_GROUPS):
            vbuf[g, pl.ds(row0, m), :] = v[:, g * gw:(g + 1) * gw]
        if subs > 1:
            @pl.when(sub == 0)
            def _():
                for g in range(N_GROUPS):
                    vbuf[g, m:subs * m, :] = jnp.zeros(((subs - 1) * m, gw), F32)
        t0 = pl.multiple_of(sub * tt, tt)
        ri = t0 + lax.broadcasted_iota(jnp.int32, (tt, clen), 0)
        ci = lax.broadcasted_iota(jnp.int32, (tt, clen), 1)
        bias = bs_ref[pl.ds(t0, tt), :]
        for g in range(N_GROUPS):
            vg = jnp.concatenate([vbuf[g, pl.ds(b, clen, stride=nb), :] for b in range(nb)], axis=1)
            wsg = jnp.where(ri >= ci, ws_ref[g, pl.ds(t0, tt), :], 0.0).astype(BF16)
            sg = _dot(wsg, vg.astype(BF16)) + bias[:, g:g + 1]
            for b in range(nb):
                sbuf[g, pl.ds(b, tt, stride=nb), :] = sg[:, b * gw:(b + 1) * gw]
        s_all = _tm_all(sbuf)

        @pl.when(j == last)
        def _():
            tv_ref[...] = _tm_all(vbuf)
    m_s[...] += gate(2) * _dot((u * s_all).astype(BF16), wco_ref[...])

    xd = seg(7 * w, 8 * w)
    ext_d = jnp.concatenate([hist_d[...], xd], axis=0)
    pos = t.past + j * tt + lax.broadcasted_iota(jnp.int32, (m, gw), 0) // nb
    pgs = []
    for gi, win in enumerate(POOL_WINDOWS):
        cs = slice(gi * gw, (gi + 1) * gw)
        acc = xd[:, cs]
        for lag in range(1, win):
            acc = acc + ext_d[hpd - lag * nb:hpd - lag * nb + m, cs]
        cnt = jnp.minimum(win, pos + 1).astype(F32)
        pgs.append(_dot((acc / cnt - xd[:, cs]).astype(BF16), wdg_ref[gi]))
    hist_d[...] = ext_d[m:m + hpd]
    pd = jnp.concatenate(pgs, axis=1) * scd_ref[...]
    m_s[...] += gate(3) * _dot(pd.astype(BF16), wdo_ref[...])

    @pl.when(j == last)
    def _():
        rd = td_ref.shape[0]
        td_ref[...] = ext_d[hpd + m - rd:hpd + m]

    _set_tm(tm, _dot(m_s[...].astype(BF16), wmix_ref[...]))
    for b in range(nb):
        x1 = _rows(x_ref, b, tt) + _from_tm(tm, b, t)
        _set_rows(xo_ref, b, tt, x1)
        _to_tm(tm2, b, _rms(x1, gx_ref[...]), t)
    scale = float(d // X_HEADS) ** -0.5
    _set_tm(tm, _dot(_tm_all(tm2).astype(BF16), wq_ref[...]) * scale)
    for b in range(nb):
        _set_rows(q_ref, b, tt, _from_tm(tm, b, t).astype(q_ref.dtype))


def _const(shape, layer):
    nz = len(shape)
    return pl.BlockSpec((None,) + tuple(shape), lambda i, j: (layer,) + (0,) * nz,
                        pipeline_mode=pl.Buffered(1))


def _mix_call(t, layer, x, sa, sb, sd, wts, q_dtype):
    (gmix, win, caw, cab, lag, lab, wao, cbw, cbb, wbo, lcg, lcb, ws, bs, wco,
     wdg, scd, wdo, wmix, gx, wq) = wts
    nb, tt = t.nb, t.tt
    m = nb * tt
    d = win.shape[1]
    w = wao.shape[1]
    ka, kb = caw.shape[1], cbw.shape[1]
    hpa, hpb, hpd = sa.shape[1], sb.shape[1], sd.shape[1]
    ra, rd = min(hpa, m), min(hpd, m)
    clen = min(t.seq_len, CHUNK)
    short_chunk = clen == tt and tt < 8
    rv = m if short_chunk else clen * nb
    if x.ndim == 3:
        x_spec = pl.BlockSpec((nb, tt, d), lambda i, j: (i, j, 0))
        x_shape = x.shape
    else:
        x_spec = pl.BlockSpec((m, d), lambda i, j: (i * t.ntile + j, 0))
        x_shape = x.shape

    def state_spec(rows):
        return pl.BlockSpec((None, rows, w), lambda i, j: (i, 0, 0))

    in_specs = [x_spec, state_spec(hpa), state_spec(hpb), state_spec(hpd)]
    in_specs += [_const(a.shape[1:], layer) for a in wts]
    out_shape = [jax.ShapeDtypeStruct(x_shape, F32), jax.ShapeDtypeStruct(x_shape, q_dtype),
                 jax.ShapeDtypeStruct((t.nblk, ra, w), F32), jax.ShapeDtypeStruct((t.nblk, hpb, w), F32),
                 jax.ShapeDtypeStruct((t.nblk, rd, w), F32), jax.ShapeDtypeStruct((t.nblk, rv, w), F32)]
    out_specs = [x_spec, x_spec, state_spec(ra), state_spec(hpb), state_spec(rd), state_spec(rv)]
    scratch = [pltpu.VMEM((d // LANES, m, LANES), F32), pltpu.VMEM((d // LANES, m, LANES), F32),
               pltpu.VMEM((m, d), BF16), pltpu.VMEM((hpa + m, w), F32),
               pltpu.VMEM((hpb, w), F32), pltpu.VMEM((hpd, w), F32),
               pltpu.VMEM((m, w), BF16), pltpu.VMEM((m, d), F32)]
    if not short_chunk:
        scratch += [pltpu.VMEM((N_GROUPS, clen * nb, w // N_GROUPS), F32),
                    pltpu.VMEM((N_GROUPS, m, w // N_GROUPS), F32)]
    return pl.pallas_call(
        functools.partial(_mix_kernel, t),
        grid=(t.nblk, t.ntile),
        in_specs=in_specs, out_specs=out_specs, out_shape=out_shape,
        scratch_shapes=scratch,
        compiler_params=pltpu.CompilerParams(
            dimension_semantics=("arbitrary", "arbitrary"), vmem_limit_bytes=VMEM_LIMIT_BYTES),
        name=f"mix_l{layer}_nb{nb}",
    )(x, sa, sb, sd, *wts)


def _head_rows(ref, b, h, nm):
    per_m = ref.shape[1] // nm
    nlb = per_m // X_HEADS
    return jnp.concatenate(
        [ref[b, pl.ds(lb * X_HEADS + h, nm, stride=per_m), :] for lb in range(nlb)], axis=1)


def _att_kernel(nm, q_ref, k_ref, v_ref, o_ref, s_scr):
    sb, tq, d = q_ref.shape
    hd = d // X_HEADS
    for b in range(sb):
        q = q_ref[b].astype(BF16)
        for h in range(X_HEADS):
            kh = _head_rows(k_ref, b, h, nm).astype(BF16)
            r0 = (b * X_HEADS + h) * tq
            s_scr[r0:r0 + tq, :] = lax.dot_general(q[:, h * hd:(h + 1) * hd], kh, (((1,), (1,)), ((), ())),
                                                   preferred_element_type=F32)
    s = s_scr[...]
    e = jnp.exp(s - jnp.max(s, axis=-1, keepdims=True))
    s_scr[...] = e / jnp.sum(e, axis=-1, keepdims=True)
    for b in range(sb):
        outs = []
        for h in range(X_HEADS):
            vh = _head_rows(v_ref, b, h, nm).astype(BF16)
            r0 = (b * X_HEADS + h) * tq
            outs.append(_dot(s_scr[r0:r0 + tq, :].astype(BF16), vh))
        o_ref[b] = jnp.concatenate(outs, axis=1).astype(o_ref.dtype)


def _att_call(q, k, v, layer, sb, tq, nm, name):
    n, tlen, d = q.shape
    rows = k.shape[2]
    q_spec = pl.BlockSpec((sb, tq, d), lambda i, j: (i, j, 0))
    kv_spec = pl.BlockSpec((None, sb, rows, LANES), lambda i, j: (layer, i, 0, 0))
    return pl.pallas_call(
        functools.partial(_att_kernel, nm),
        grid=(n // sb, tlen // tq),
        in_specs=[q_spec, kv_spec, kv_spec], out_specs=q_spec,
        out_shape=jax.ShapeDtypeStruct(q.shape, q.dtype),
        scratch_shapes=[pltpu.VMEM((sb * X_HEADS * tq, nm), F32)],
        compiler_params=pltpu.CompilerParams(
            dimension_semantics=("arbitrary", "arbitrary"), vmem_limit_bytes=VMEM_LIMIT_BYTES),
        name=name,
    )(q, k, v)


def _ffn_kernel(t, final, x_ref, o_ref, sf_ref, wxo_ref, gffn_ref, wup_ref, fcw_ref, fcb_ref,
                wdown_ref, gfin_ref, xo_ref, tf_ref, tm, h_s, hist_f, act_s):
    nb, tt = t.nb, t.tt
    m = nb * tt
    d = h_s.shape[1]
    dff = act_s.shape[1]
    kf = fcw_ref.shape[0]
    hp = (kf - 1) * nb
    j = pl.program_id(1)

    @pl.when(j == 0)
    def _():
        hist_f[...] = sf_ref[...]

    o2 = o_ref[...]
    if len(o2.shape) == 3:
        o2 = o2.reshape(m, d)
    ao = _dot(o2.astype(BF16), wxo_ref[...])
    for b in range(nb):
        x2 = _rows(x_ref, b, tt) + ao[b * tt:(b + 1) * tt]
        _set_rows(xo_ref, b, tt, x2)
        _to_tm(tm, b, _rms(x2, gffn_ref[...]), t)
    h_s[...] = _tm_all(tm).astype(BF16)

    cw = 256
    for c in range(dff // cw):
        parts = []
        for col0 in (c * cw, dff + c * cw):
            cs = slice(col0, col0 + cw)
            up = _dot(h_s[...], wup_ref[:, cs])
            ext = jnp.concatenate([hist_f[:, cs], up], axis=0)
            cv = fcb_ref[:, cs] + fcw_ref[0:1, cs] * ext[0:m]
            for k in range(1, kf):
                cv = cv + fcw_ref[k:k + 1, cs] * ext[k * nb:k * nb + m]
            hist_f[:, cs] = ext[m:m + hp]
            parts.append(cv)
        act_s[:, c * cw:(c + 1) * cw] = (parts[0] * jax.nn.sigmoid(parts[0]) * parts[1]).astype(BF16)

    @pl.when(j == t.ntile - 1)
    def _():
        tf_ref[...] = hist_f[...]

    _set_tm(tm, _dot(act_s[...], wdown_ref[...]))
    for b in range(nb):
        x3 = _rows(xo_ref, b, tt) + _from_tm(tm, b, t)
        if final:
            x3 = _rms(x3, gfin_ref[...])
        _set_rows(xo_ref, b, tt, x3)


def _ffn_call(t, layer, final, x, o, sf, wxo, gffn, wup, fcw, fcb, wdown, gfin):
    nb, tt = t.nb, t.tt
    m = nb * tt
    d = wxo.shape[1]
    dff = wdown.shape[1]
    hp = sf.shape[1]
    if x.ndim == 3:
        x_spec = pl.BlockSpec((nb, tt, d), lambda i, j: (i, j, 0))
    else:
        x_spec = pl.BlockSpec((m, d), lambda i, j: (i * t.ntile + j, 0))
    sf_spec = pl.BlockSpec((None, hp, 2 * dff), lambda i, j: (i, 0, 0))
    wts = (wxo, gffn, wup, fcw, fcb, wdown)
    in_specs = [x_spec, x_spec, sf_spec] + [_const(a.shape[1:], layer) for a in wts]
    in_specs.append(pl.BlockSpec(gfin.shape, lambda i, j: (0, 0), pipeline_mode=pl.Buffered(1)))
    return pl.pallas_call(
        functools.partial(_ffn_kernel, t, final),
        grid=(t.nblk, t.ntile),
        in_specs=in_specs, out_specs=[x_spec, sf_spec],
        out_shape=[jax.ShapeDtypeStruct(x.shape, F32), jax.ShapeDtypeStruct((t.nblk, hp, 2 * dff), F32)],
        scratch_shapes=[pltpu.VMEM((d // LANES, m, LANES), F32), pltpu.VMEM((m, d), BF16),
                        pltpu.VMEM((hp, 2 * dff), F32), pltpu.VMEM((m, dff), BF16)],
        compiler_params=pltpu.CompilerParams(
            dimension_semantics=("arbitrary", "arbitrary"), vmem_limit_bytes=VMEM_LIMIT_BYTES),
        name=f"ffn_l{layer}_nb{nb}",
    )(x, o, sf, *wts, gfin)


def _kv_kernel(x_ref, wk_ref, wv_ref, k_ref, v_ref):
    x = x_ref[...].astype(BF16)
    bm, d = x.shape
    hd = d // X_HEADS
    nlb = hd // LANES
    per_m = nlb * X_HEADS
    for w_ref, out_ref in ((wk_ref, k_ref), (wv_ref, v_ref)):
        y = _dot(x, w_ref[...])
        for h in range(X_HEADS):
            for lb in range(nlb):
                c0 = h * hd + lb * LANES
                out_ref[pl.ds(lb * X_HEADS + h, bm, stride=per_m), :] = y[:, c0:c0 + LANES]


def _kv_call(mem2d, wk, wv, layer):
    rows, d = mem2d.shape
    bm = 512
    per_m = d // LANES
    r_spec = pl.BlockSpec((bm, d), lambda i: (i, 0))
    o_spec = pl.BlockSpec((bm * per_m, LANES), lambda i: (i, 0))
    w_spec = pl.BlockSpec((None, d, d), lambda i: (layer, 0, 0), pipeline_mode=pl.Buffered(1))
    return pl.pallas_call(
        _kv_kernel, grid=(rows // bm,),
        in_specs=[r_spec, w_spec, w_spec], out_specs=[o_spec, o_spec],
        out_shape=[jax.ShapeDtypeStruct((rows * per_m, LANES), F32)] * 2,
        compiler_params=pltpu.CompilerParams(dimension_semantics=("arbitrary",)),
        name=f"kv_l{layer}",
    )(mem2d, wk, wv)


def _state_to_tm(state, t):
    n, h, c = state.shape
    return state.reshape(t.nblk, t.nb, h, c).transpose(0, 2, 1, 3).reshape(t.nblk, h * t.nb, c)


def _mem_view(c):
    *lead, nm, h, hd = c.shape
    nlb = hd // LANES
    n = len(lead)
    c = c.reshape(*lead, nm, h, nlb, LANES)
    c = c.transpose(*range(n), n, n + 2, n + 1, n + 3)
    return c.reshape(*lead, nm * nlb * h, LANES)


def _mem_unview(g, nm, h):
    *lead, rows, _ = g.shape
    nlb = rows // (nm * h)
    n = len(lead)
    g = g.reshape(*lead, nm, nlb, h, LANES)
    g = g.transpose(*range(n), n, n + 2, n + 1, n + 3)
    return g.reshape(*lead, nm, h, nlb * LANES)


def _tail_to_state(tail, old, t, h):
    nblk, r, c = tail.shape
    k = r // t.nb
    new = tail.reshape(nblk, k, t.nb, c).transpose(0, 2, 1, 3).reshape(nblk * t.nb, k, c)
    if k < h:
        new = jnp.concatenate([old[:, k:], new], axis=1)
    return new


def kernel(x_prompt, x_sample, mem_prompt, state_conv_a, state_conv_b, state_pool_d, state_ffn_conv,
           cache_mem_k, cache_mem_v, g_mix, w_in, conv_a_w, conv_a_b, ln_a_g, ln_a_b, w_a_out,
           conv_b_w, conv_b_b, w_b_out, ln_c_g, ln_c_b, w_s, b_s, w_c_out, w_d_grp, scale_d, w_d_out,
           w_mix_out, g_xattn, w_q, w_mk, w_mv, w_xo, g_ffn, w_up, ffn_conv_w, ffn_conv_b, w_down, g_final):
    depth = w_in.shape[0]
    nbp, seq, d = x_prompt.shape
    nbs, dseq, _ = x_sample.shape
    w = w_a_out.shape[1]
    nm = mem_prompt.shape[1]
    ha, hb, hd, hf = conv_a_w.shape[1] - 1, conv_b_w.shape[1] - 1, max(POOL_WINDOWS) - 1, ffn_conv_w.shape[1] - 1

    tp = Tile(nb=nbp, tt=64, nblk=1, ntile=seq // 64, past=0, seq_len=seq)
    ts = Tile(nb=32, tt=dseq, nblk=nbs // 32, ntile=1, past=PAST_LEN, seq_len=dseq)

    row = lambda a: a.reshape(depth, 1, a.shape[-1])
    bf = lambda a: a.astype(BF16)
    common = dict(
        gmix=row(g_mix), win=bf(w_in), caw=conv_a_w, cab=row(conv_a_b), lag=row(ln_a_g), lab=row(ln_a_b),
        wao=bf(w_a_out), cbw=conv_b_w, cbb=row(conv_b_b), wbo=bf(w_b_out), lcg=row(ln_c_g), lcb=row(ln_c_b),
        wco=bf(w_c_out), wdg=bf(w_d_grp), scd=row(scale_d), wdo=bf(w_d_out), wmix=bf(w_mix_out),
        gx=row(g_xattn), wq=bf(w_q))
    order = ("gmix", "win", "caw", "cab", "lag", "lab", "wao", "cbw", "cbb", "wbo", "lcg", "lcb",
             "ws", "bs", "wco", "wdg", "scd", "wdo", "wmix", "gx", "wq")
    wts_p = dict(common, ws=w_s, bs=b_s.transpose(0, 2, 1))
    gw = w // N_GROUPS
    ws_x = jnp.repeat(w_s[:, :, :dseq, :dseq].transpose(0, 2, 3, 1), gw, axis=-1)
    bs_x = jnp.repeat(b_s[:, :, :dseq].transpose(0, 2, 1), gw, axis=-1)
    wts_s = dict(common, ws=ws_x, bs=bs_x)
    wts_p = tuple(wts_p[k] for k in order)
    wts_s = tuple(wts_s[k] for k in order)

    wxo_b, wup_b, wdown_b, wmk_b, wmv_b = bf(w_xo), bf(w_up), bf(w_down), bf(w_mk), bf(w_mv)
    gffn_r, fcb_r, gfin_r = row(g_ffn), row(ffn_conv_b), g_final.reshape(1, d)

    xp = x_prompt
    xs = x_sample.reshape(nbs * dseq, d)
    mem2d = mem_prompt.reshape(nbp * nm, d)
    ck, cv = _mem_view(cache_mem_k), _mem_view(cache_mem_v)
    zeros = lambda h, c: jnp.zeros((1, h * nbp, c), F32)

    outs = {k: [] for k in ("a_p", "b_p", "d_p", "f_p", "v_p", "mk", "mv", "a_s", "b_s", "d_s", "f_s", "v_s")}
    for l in range(depth):
        final = l == depth - 1
        mk, mv = _kv_call(mem2d, wmk_b, wmv_b, l)
        xp, q, ta, tb, td, tv = _mix_call(tp, l, xp, zeros(ha, w), zeros(hb, w), zeros(hd, w), wts_p, BF16)
        mk, mv = mk.reshape(1, nbp, -1, LANES), mv.reshape(1, nbp, -1, LANES)
        o = _att_call(q, mk, mv, 0, 1, min(seq, 512), nm, f"att_p_l{l}")
        xp, tf = _ffn_call(tp, l, final, xp, o, zeros(hf, ffn_conv_w.shape[2]), wxo_b, gffn_r, wup_b,
                           ffn_conv_w, fcb_r, wdown_b, gfin_r)
        outs["a_p"].append(_tail_to_state(ta, None, tp, ha))
        outs["b_p"].append(_tail_to_state(tb, None, tp, hb))
        outs["d_p"].append(_tail_to_state(td, None, tp, hd))
        outs["f_p"].append(_tail_to_state(tf, None, tp, hf))
        outs["v_p"].append(_tail_to_state(tv, None, tp, CHUNK))
        outs["mk"].append(_mem_unview(mk[0], nm, X_HEADS))
        outs["mv"].append(_mem_unview(mv[0], nm, X_HEADS))
        sa, sb_, sd, sf = state_conv_a[l], state_conv_b[l], state_pool_d[l], state_ffn_conv[l]
        xs, q, ta, tb, td, tv = _mix_call(ts, l, xs, _state_to_tm(sa, ts), _state_to_tm(sb_, ts),
                                          _state_to_tm(sd, ts), wts_s, F32)
        o = _att_call(q.reshape(nbs, dseq, d), ck, cv, l, 8, dseq, nm, f"att_s_l{l}")
        xs, tf = _ffn_call(ts, l, final, xs, o.reshape(nbs * dseq, d), _state_to_tm(sf, ts), wxo_b, gffn_r,
                           wup_b, ffn_conv_w, fcb_r, wdown_b, gfin_r)
        outs["a_s"].append(_tail_to_state(ta, sa, ts, ha))
        outs["b_s"].append(_tail_to_state(tb, sb_, ts, hb))
        outs["d_s"].append(_tail_to_state(td, sd, ts, hd))
        outs["f_s"].append(_tail_to_state(tf, sf, ts, hf))
        outs["v_s"].append(_tail_to_state(tv, None, ts, dseq))

    st = lambda k: jnp.stack(outs[k])
    return (xp, xs.reshape(nbs, dseq, d),
            st("a_p"), st("b_p"), st("d_p"), st("f_p"), st("v_p"), st("mk"), st("mv"),
            st("a_s"), st("b_s"), st("d_s"), st("f_s"), st("v_s"))
```

```python
import functools
from typing import NamedTuple

import jax
import jax.numpy as jnp
from jax import lax
from jax.experimental import pallas as pl
from jax.experimental.pallas import tpu as pltpu

F32 = jnp.float32
BF16 = jnp.bfloat16

EPS = 1e-6
LANES = 128
CHUNK = 128
POOL_WINDOWS = (2, 4, 8, 16)
N_GROUPS = 4
X_HEADS = 4
PAST_LEN = 16384
VMEM_LIMIT_BYTES = 56 * 1024 * 1024


class Tile(NamedTuple):
    nb: int
    tt: int
    nblk: int
    ntile: int
    past: int
    seq_len: int


def _rms(x, g):
    ms = jnp.mean(x * x, axis=-1, keepdims=True)
    return x * lax.rsqrt(ms + EPS) * g


def _ln(x, g, b):
    mu = jnp.mean(x, axis=-1, keepdims=True)
    xc = x - mu
    var = jnp.mean(xc * xc, axis=-1, keepdims=True)
    return xc * lax.rsqrt(var + EPS) * g + b


def _dot(a, b):
    return jnp.dot(a, b, preferred_element_type=F32)


def _rows(ref, b, tt):
    if len(ref.shape) == 3:
        return ref[b]
    return ref[b * tt:(b + 1) * tt, :]


def _set_rows(ref, b, tt, val):
    if len(ref.shape) == 3:
        ref[b] = val
    else:
        ref[b * tt:(b + 1) * tt, :] = val


def _to_tm(tm, b, val, t):
    for c in range(val.shape[1] // LANES):
        tm[c, pl.ds(b, t.tt, stride=t.nb), :] = val[:, c * LANES:(c + 1) * LANES]


def _from_tm(tm, b, t):
    return jnp.concatenate([tm[c, pl.ds(b, t.tt, stride=t.nb), :] for c in range(tm.shape[0])], axis=1)


def _tm_all(tm):
    return jnp.concatenate([tm[c] for c in range(tm.shape[0])], axis=1)


def _set_tm(tm, val):
    for c in range(tm.shape[0]):
        tm[c] = val[:, c * LANES:(c + 1) * LANES]


def _mix_kernel(t, x_ref, sa_ref, sb_ref, sd_ref,
                gmix_ref, win_ref, caw_ref, cab_ref, lag_ref, lab_ref, wao_ref,
                cbw_ref, cbb_ref, wbo_ref, lcg_ref, lcb_ref, ws_ref, bs_ref, wco_ref,
                wdg_ref, scd_ref, wdo_ref, wmix_ref, gx_ref, wq_ref,
                xo_ref, q_ref, ta_ref, tb_ref, td_ref, tv_ref,
                tm, tm2, h_s, ext_a, hist_b, hist_d, act_s, m_s, *sbufs):
    nb, tt = t.nb, t.tt
    m = nb * tt
    d = h_s.shape[1]
    w = act_s.shape[1]
    ka = caw_ref.shape[0]
    kb = cbw_ref.shape[0]
    hpa, hpb, hpd = (ka - 1) * nb, (kb - 1) * nb, hist_d.shape[0]
    gw = w // N_GROUPS
    clen = min(t.seq_len, CHUNK)
    short_chunk = clen == tt and tt < 8
    j = pl.program_id(1)

    @pl.when(j == 0)
    def _():
        ext_a[0:hpa, :] = sa_ref[...]
        hist_b[...] = sb_ref[...]
        hist_d[...] = sd_ref[...]
        if not short_chunk:
            tv_ref[...] = jnp.zeros(tv_ref.shape, F32)

    for b in range(nb):
        _to_tm(tm, b, _rms(_rows(x_ref, b, tt), gmix_ref[...]), t)
    h_s[...] = _tm_all(tm).astype(BF16)

    def seg(lo, hi):
        return _dot(h_s[...], win_ref[:, lo:hi])

    gate0 = 8 * w

    def gate(i):
        return jax.nn.sigmoid(seg(gate0 + i * d, gate0 + (i + 1) * d))

    za = seg(0, 2 * w)
    ext_a[hpa:hpa + m, :] = za[:, :w] * jax.nn.sigmoid(za[:, w:])
    rc = 32
    for r0 in range(0, m, rc):
        acc = jnp.broadcast_to(cab_ref[...], (rc, w))
        for k in range(ka):
            acc = acc + caw_ref[k:k + 1, :] * ext_a[r0 + k * nb:r0 + k * nb + rc, :]
        y = _ln(acc, lag_ref[...], lab_ref[...])
        act_s[r0:r0 + rc, :] = (y * jax.nn.sigmoid(y)).astype(BF16)
    m_s[...] = gate(0) * _dot(act_s[...], wao_ref[...])
    ra = ta_ref.shape[0]
    ta_ref[...] = ext_a[hpa + m - ra:hpa + m, :]
    if t.ntile > 1:
        ext_a[0:hpa, :] = ext_a[m:m + hpa, :]

    zb = seg(2 * w, 5 * w)
    ext_b = jnp.concatenate([hist_b[...], zb[:, w:2 * w] * zb[:, 2 * w:3 * w]], axis=0)
    cb = cbb_ref[...] + cbw_ref[0:1, :] * ext_b[0:m]
    for k in range(1, kb):
        cb = cb + cbw_ref[k:k + 1, :] * ext_b[k * nb:k * nb + m]
    hist_b[...] = ext_b[m:m + hpb]
    tb_ref[...] = ext_b[m:m + hpb]
    m_s[...] += gate(1) * _dot((zb[:, :w] * cb).astype(BF16), wbo_ref[...])

    zc = jax.nn.gelu(seg(5 * w, 7 * w))
    u = zc[:, :w]
    v = _ln(zc[:, w:], lcg_ref[...], lcb_ref[...])
    if short_chunk:
        s_rows = []
        for ti in range(tt):
            s_t = jnp.broadcast_to(bs_ref[ti:ti + 1, :], (nb, w))
            for si in range(ti + 1):
                s_t = s_t + ws_ref[ti, si:si + 1, :] * v[si * nb:(si + 1) * nb]
            s_rows.append(s_t)
        s_all = jnp.concatenate(s_rows, axis=0)
        tv_ref[...] = v
    else:
        vbuf, (sbuf,) = tv_ref, sbufs
        subs = clen // tt
        sub = j % subs
        row0 = pl.multiple_of(sub * m, m)
        for g in range(N_GROUPS):
            vbuf[g, pl.ds(row0, m), :] = v[:, g * gw:(g + 1) * gw]
        t0 = pl.multiple_of(sub * tt, tt)
        ri = t0 + lax.broadcasted_iota(jnp.int32, (tt, clen), 0)
        ci = lax.broadcasted_iota(jnp.int32, (tt, clen), 1)
        bias = bs_ref[pl.ds(t0, tt), :]
        for g in range(N_GROUPS):
            vg = jnp.concatenate([vbuf[g, pl.ds(b, clen, stride=nb), :] for b in range(nb)], axis=1)
            wsg = jnp.where(ri >= ci, ws_ref[g, pl.ds(t0, tt), :], 0.0).astype(BF16)
            sg = _dot(wsg, vg.astype(BF16)) + bias[:, g:g + 1]
            for b in range(nb):
                sbuf[g, pl.ds(b, tt, stride=nb), :] = sg[:, b * gw:(b + 1) * gw]
        s_all = _tm_all(sbuf)
    m_s[...] += gate(2) * _dot((u * s_all).astype(BF16), wco_ref[...])

    xd = seg(7 * w, 8 * w)
    ext_d = jnp.concatenate([hist_d[...], xd], axis=0)
    pos = t.past + j * tt + lax.broadcasted_iota(jnp.int32, (m, gw), 0) // nb
    pgs = []
    for gi, win in enumerate(POOL_WINDOWS):
        cs = slice(gi * gw, (gi + 1) * gw)
        acc = xd[:, cs]
        for lag in range(1, win):
            acc = acc + ext_d[hpd - lag * nb:hpd - lag * nb + m, cs]
        cnt = jnp.minimum(win, pos + 1).astype(F32)
        pgs.append(_dot((acc / cnt - xd[:, cs]).astype(BF16), wdg_ref[gi]))
    hist_d[...] = ext_d[m:m + hpd]
    pd = jnp.concatenate(pgs, axis=1) * scd_ref[...]
    m_s[...] += gate(3) * _dot(pd.astype(BF16), wdo_ref[...])
    rd = td_ref.shape[0]
    td_ref[...] = ext_d[hpd + m - rd:hpd + m]

    _set_tm(tm, _dot(m_s[...].astype(BF16), wmix_ref[...]))
    for b in range(nb):
        x1 = _rows(x_ref, b, tt) + _from_tm(tm, b, t)
        _set_rows(xo_ref, b, tt, x1)
        _to_tm(tm2, b, _rms(x1, gx_ref[...]), t)
    scale = float(d // X_HEADS) ** -0.5
    _set_tm(tm, _dot(_tm_all(tm2).astype(BF16), wq_ref[...]) * scale)
    for b in range(nb):
        _set_rows(q_ref, b, tt, _from_tm(tm, b, t).astype(q_ref.dtype))


def _const(shape, layer):
    nz = len(shape)
    return pl.BlockSpec((None,) + tuple(shape), lambda i, j: (layer,) + (0,) * nz,
                        pipeline_mode=pl.Buffered(1))


def _mix_call(t, layer, x, sa, sb, sd, wts, q_dtype):
    (gmix, win, caw, cab, lag, lab, wao, cbw, cbb, wbo, lcg, lcb, ws, bs, wco,
     wdg, scd, wdo, wmix, gx, wq) = wts
    nb, tt = t.nb, t.tt
    m = nb * tt
    d = win.shape[1]
    w = wao.shape[1]
    ka, kb = caw.shape[1], cbw.shape[1]
    hpa, hpb, hpd = sa.shape[1], sb.shape[1], sd.shape[1]
    ra, rd = min(hpa, m), min(hpd, m)
    clen = min(t.seq_len, CHUNK)
    short_chunk = clen == tt and tt < 8
    gw = w // N_GROUPS
    if x.ndim == 3:
        x_spec = pl.BlockSpec((nb, tt, d), lambda i, j: (i, j, 0))
        x_shape = x.shape
    else:
        x_spec = pl.BlockSpec((m, d), lambda i, j: (i * t.ntile + j, 0))
        x_shape = x.shape

    def state_spec(rows):
        return pl.BlockSpec((None, rows, w), lambda i, j: (i, 0, 0))

    in_specs = [x_spec, state_spec(hpa), state_spec(hpb), state_spec(hpd)]
    in_specs += [_const(a.shape[1:], layer) for a in wts]
    out_shape = [jax.ShapeDtypeStruct(x_shape, F32), jax.ShapeDtypeStruct(x_shape, q_dtype),
                 jax.ShapeDtypeStruct((t.nblk, ra, w), F32), jax.ShapeDtypeStruct((t.nblk, hpb, w), F32),
                 jax.ShapeDtypeStruct((t.nblk, rd, w), F32)]
    out_specs = [x_spec, x_spec, state_spec(ra), state_spec(hpb), state_spec(rd)]
    if short_chunk:
        out_shape.append(jax.ShapeDtypeStruct((t.nblk, m, w), F32))
        out_specs.append(state_spec(m))
    else:
        out_shape.append(jax.ShapeDtypeStruct((t.nblk, N_GROUPS, clen * nb, gw), F32))
        out_specs.append(pl.BlockSpec((None, N_GROUPS, clen * nb, gw), lambda i, j: (i, 0, 0, 0)))
    scratch = [pltpu.VMEM((d // LANES, m, LANES), F32), pltpu.VMEM((d // LANES, m, LANES), F32),
               pltpu.VMEM((m, d), BF16), pltpu.VMEM((hpa + m, w), F32),
               pltpu.VMEM((hpb, w), F32), pltpu.VMEM((hpd, w), F32),
               pltpu.VMEM((m, w), BF16), pltpu.VMEM((m, d), F32)]
    if not short_chunk:
        scratch.append(pltpu.VMEM((N_GROUPS, m, gw), F32))
    return pl.pallas_call(
        functools.partial(_mix_kernel, t),
        grid=(t.nblk, t.ntile),
        in_specs=in_specs, out_specs=out_specs, out_shape=out_shape,
        scratch_shapes=scratch,
        compiler_params=pltpu.CompilerParams(
            dimension_semantics=("arbitrary", "arbitrary"), vmem_limit_bytes=VMEM_LIMIT_BYTES),
        name=f"mix_l{layer}_nb{nb}",
    )(x, sa, sb, sd, *wts)


def _head_rows(ref, b, h, nm):
    per_m = ref.shape[1] // nm
    nlb = per_m // X_HEADS
    return jnp.concatenate(
        [ref[b, pl.ds(lb * X_HEADS + h, nm, stride=per_m), :] for lb in range(nlb)], axis=1)


def _att_kernel(nm, q_ref, k_ref, v_ref, o_ref, s_scr):
    sb, tq, d = q_ref.shape
    hd = d // X_HEADS
    for b in range(sb):
        q = q_ref[b].astype(BF16)
        for h in range(X_HEADS):
            kh = _head_rows(k_ref, b, h, nm).astype(BF16)
            r0 = (b * X_HEADS + h) * tq
            s_scr[r0:r0 + tq, :] = lax.dot_general(q[:, h * hd:(h + 1) * hd], kh, (((1,), (1,)), ((), ())),
                                                   preferred_element_type=F32)
    s = s_scr[...]
    e = jnp.exp(s - jnp.max(s, axis=-1, keepdims=True))
    s_scr[...] = e / jnp.sum(e, axis=-1, keepdims=True)
    for b in range(sb):
        outs = []
        for h in range(X_HEADS):
            vh = _head_rows(v_ref, b, h, nm).astype(BF16)
            r0 = (b * X_HEADS + h) * tq
            outs.append(_dot(s_scr[r0:r0 + tq, :].astype(BF16), vh))
        o_ref[b] = jnp.concatenate(outs, axis=1).astype(o_ref.dtype)


def _att_call(q, k, v, layer, sb, tq, nm, name):
    n, tlen, d = q.shape
    rows = k.shape[2]
    q_spec = pl.BlockSpec((sb, tq, d), lambda i, j: (i, j, 0))
    kv_spec = pl.BlockSpec((None, sb, rows, LANES), lambda i, j: (layer, i, 0, 0))
    return pl.pallas_call(
        functools.partial(_att_kernel, nm),
        grid=(n // sb, tlen // tq),
        in_specs=[q_spec, kv_spec, kv_spec], out_specs=q_spec,
        out_shape=jax.ShapeDtypeStruct(q.shape, q.dtype),
        scratch_shapes=[pltpu.VMEM((sb * X_HEADS * tq, nm), F32)],
        compiler_params=pltpu.CompilerParams(
            dimension_semantics=("arbitrary", "arbitrary"), vmem_limit_bytes=VMEM_LIMIT_BYTES),
        name=name,
    )(q, k, v)


def _ffn_kernel(t, final, x_ref, o_ref, sf_ref, wxo_ref, gffn_ref, wup_ref, fcw_ref, fcb_ref,
                wdown_ref, gfin_ref, xo_ref, tf_ref, tm, h_s, hist_f, act_s):
    nb, tt = t.nb, t.tt
    m = nb * tt
    d = h_s.shape[1]
    dff = act_s.shape[1]
    kf = fcw_ref.shape[0]
    hp = (kf - 1) * nb
    j = pl.program_id(1)

    @pl.when(j == 0)
    def _():
        hist_f[...] = sf_ref[...]

    o2 = o_ref[...]
    if len(o2.shape) == 3:
        o2 = o2.reshape(m, d)
    ao = _dot(o2.astype(BF16), wxo_ref[...])
    for b in range(nb):
        x2 = _rows(x_ref, b, tt) + ao[b * tt:(b + 1) * tt]
        _set_rows(xo_ref, b, tt, x2)
        _to_tm(tm, b, _rms(x2, gffn_ref[...]), t)
    h_s[...] = _tm_all(tm).astype(BF16)

    cw = 256
    for c in range(dff // cw):
        parts = []
        for col0 in (c * cw, dff + c * cw):
            cs = slice(col0, col0 + cw)
            up = _dot(h_s[...], wup_ref[:, cs])
            ext = jnp.concatenate([hist_f[:, cs], up], axis=0)
            cv = fcb_ref[:, cs] + fcw_ref[0:1, cs] * ext[0:m]
            for k in range(1, kf):
                cv = cv + fcw_ref[k:k + 1, cs] * ext[k * nb:k * nb + m]
            hist_f[:, cs] = ext[m:m + hp]
            parts.append(cv)
        act_s[:, c * cw:(c + 1) * cw] = (parts[0] * jax.nn.sigmoid(parts[0]) * parts[1]).astype(BF16)

    tf_ref[...] = hist_f[...]
    _set_tm(tm, _dot(act_s[...], wdown_ref[...]))
    for b in range(nb):
        x3 = _rows(xo_ref, b, tt) + _from_tm(tm, b, t)
        if final:
            x3 = _rms(x3, gfin_ref[...])
        _set_rows(xo_ref, b, tt, x3)


def _ffn_call(t, layer, final, x, o, sf, wxo, gffn, wup, fcw, fcb, wdown, gfin):
    nb, tt = t.nb, t.tt
    m = nb * tt
    d = wxo.shape[1]
    dff = wdown.shape[1]
    hp = sf.shape[1]
    if x.ndim == 3:
        x_spec = pl.BlockSpec((nb, tt, d), lambda i, j: (i, j, 0))
    else:
        x_spec = pl.BlockSpec((m, d), lambda i, j: (i * t.ntile + j, 0))
    sf_spec = pl.BlockSpec((None, hp, 2 * dff), lambda i, j: (i, 0, 0))
    wts = (wxo, gffn, wup, fcw, fcb, wdown)
    in_specs = [x_spec, x_spec, sf_spec] + [_const(a.shape[1:], layer) for a in wts]
    in_specs.append(pl.BlockSpec(gfin.shape, lambda i, j: (0, 0), pipeline_mode=pl.Buffered(1)))
    return pl.pallas_call(
        functools.partial(_ffn_kernel, t, final),
        grid=(t.nblk, t.ntile),
        in_specs=in_specs, out_specs=[x_spec, sf_spec],
        out_shape=[jax.ShapeDtypeStruct(x.shape, F32), jax.ShapeDtypeStruct((t.nblk, hp, 2 * dff), F32)],
        scratch_shapes=[pltpu.VMEM((d // LANES, m, LANES), F32), pltpu.VMEM((m, d), BF16),
                        pltpu.VMEM((hp, 2 * dff), F32), pltpu.VMEM((m, dff), BF16)],
        compiler_params=pltpu.CompilerParams(
            dimension_semantics=("arbitrary", "arbitrary"), vmem_limit_bytes=VMEM_LIMIT_BYTES),
        name=f"ffn_l{layer}_nb{nb}",
    )(x, o, sf, *wts, gfin)


def _kv_kernel(x_ref, wk_ref, wv_ref, k_ref, v_ref):
    x = x_ref[...].astype(BF16)
    bm, d = x.shape
    hd = d // X_HEADS
    nlb = hd // LANES
    per_m = nlb * X_HEADS
    for w_ref, out_ref in ((wk_ref, k_ref), (wv_ref, v_ref)):
        y = _dot(x, w_ref[...])
        for h in range(X_HEADS):
            for lb in range(nlb):
                c0 = h * hd + lb * LANES
                out_ref[pl.ds(lb * X_HEADS + h, bm, stride=per_m), :] = y[:, c0:c0 + LANES]


def _kv_call(mem2d, wk, wv, layer):
    rows, d = mem2d.shape
    bm = 512
    per_m = d // LANES
    r_spec = pl.BlockSpec((bm, d), lambda i: (i, 0))
    o_spec = pl.BlockSpec((bm * per_m, LANES), lambda i: (i, 0))
    w_spec = pl.BlockSpec((None, d, d), lambda i: (layer, 0, 0), pipeline_mode=pl.Buffered(1))
    return pl.pallas_call(
        _kv_kernel, grid=(rows // bm,),
        in_specs=[r_spec, w_spec, w_spec], out_specs=[o_spec, o_spec],
        out_shape=[jax.ShapeDtypeStruct((rows * per_m, LANES), F32)] * 2,
        compiler_params=pltpu.CompilerParams(dimension_semantics=("arbitrary",)),
        name=f"kv_l{layer}",
    )(mem2d, wk, wv)


def _state_to_tm(state, t):
    n, h, c = state.shape
    return state.reshape(t.nblk, t.nb, h, c).transpose(0, 2, 1, 3).reshape(t.nblk, h * t.nb, c)


def _mem_view(c):
    *lead, nm, h, hd = c.shape
    nlb = hd // LANES
    n = len(lead)
    c = c.reshape(*lead, nm, h, nlb, LANES)
    c = c.transpose(*range(n), n, n + 2, n + 1, n + 3)
    return c.reshape(*lead, nm * nlb * h, LANES)


def _mem_unview(g, nm, h):
    *lead, rows, _ = g.shape
    nlb = rows // (nm * h)
    n = len(lead)
    g = g.reshape(*lead, nm, nlb, h, LANES)
    g = g.transpose(*range(n), n, n + 2, n + 1, n + 3)
    return g.reshape(*lead, nm, h, nlb * LANES)


def _tail_to_state(tail, old, t, h):
    nblk, r, c = tail.shape
    k = r // t.nb
    new = tail.reshape(nblk, k, t.nb, c).transpose(0, 2, 1, 3).reshape(nblk * t.nb, k, c)
    if k < h:
        new = jnp.concatenate([old[:, k:], new], axis=1)
    return new


def kernel(x_prompt, x_sample, mem_prompt, state_conv_a, state_conv_b, state_pool_d, state_ffn_conv,
           cache_mem_k, cache_mem_v, g_mix, w_in, conv_a_w, conv_a_b, ln_a_g, ln_a_b, w_a_out,
           conv_b_w, conv_b_b, w_b_out, ln_c_g, ln_c_b, w_s, b_s, w_c_out, w_d_grp, scale_d, w_d_out,
           w_mix_out, g_xattn, w_q, w_mk, w_mv, w_xo, g_ffn, w_up, ffn_conv_w, ffn_conv_b, w_down, g_final):
    depth = w_in.shape[0]
    nbp, seq, d = x_prompt.shape
    nbs, dseq, _ = x_sample.shape
    w = w_a_out.shape[1]
    nm = mem_prompt.shape[1]
    ha, hb, hd, hf = conv_a_w.shape[1] - 1, conv_b_w.shape[1] - 1, max(POOL_WINDOWS) - 1, ffn_conv_w.shape[1] - 1

    tp = Tile(nb=nbp, tt=64, nblk=1, ntile=seq // 64, past=0, seq_len=seq)
    ts = Tile(nb=32, tt=dseq, nblk=nbs // 32, ntile=1, past=PAST_LEN, seq_len=dseq)

    row = lambda a: a.reshape(depth, 1, a.shape[-1])
    bf = lambda a: a.astype(BF16)
    common = dict(
        gmix=row(g_mix), win=bf(w_in), caw=conv_a_w, cab=row(conv_a_b), lag=row(ln_a_g), lab=row(ln_a_b),
        wao=bf(w_a_out), cbw=conv_b_w, cbb=row(conv_b_b), wbo=bf(w_b_out), lcg=row(ln_c_g), lcb=row(ln_c_b),
        wco=bf(w_c_out), wdg=bf(w_d_grp), scd=row(scale_d), wdo=bf(w_d_out), wmix=bf(w_mix_out),
        gx=row(g_xattn), wq=bf(w_q))
    order = ("gmix", "win", "caw", "cab", "lag", "lab", "wao", "cbw", "cbb", "wbo", "lcg", "lcb",
             "ws", "bs", "wco", "wdg", "scd", "wdo", "wmix", "gx", "wq")
    wts_p = dict(common, ws=w_s, bs=b_s.transpose(0, 2, 1))
    gw = w // N_GROUPS
    ws_x = jnp.repeat(w_s[:, :, :dseq, :dseq].transpose(0, 2, 3, 1), gw, axis=-1)
    bs_x = jnp.repeat(b_s[:, :, :dseq].transpose(0, 2, 1), gw, axis=-1)
    wts_s = dict(common, ws=ws_x, bs=bs_x)
    wts_p = tuple(wts_p[k] for k in order)
    wts_s = tuple(wts_s[k] for k in order)

    wxo_b, wup_b, wdown_b, wmk_b, wmv_b = bf(w_xo), bf(w_up), bf(w_down), bf(w_mk), bf(w_mv)
    gffn_r, fcb_r, gfin_r = row(g_ffn), row(ffn_conv_b), g_final.reshape(1, d)

    xp = x_prompt
    xs = x_sample.reshape(nbs * dseq, d)
    mem2d = mem_prompt.reshape(nbp * nm, d)
    ck, cv = _mem_view(cache_mem_k), _mem_view(cache_mem_v)
    zeros = lambda h, c: jnp.zeros((1, h * nbp, c), F32)

    outs = {k: [] for k in ("a_p", "b_p", "d_p", "f_p", "v_p", "mk", "mv", "a_s", "b_s", "d_s", "f_s", "v_s")}
    for l in range(depth):
        final = l == depth - 1
        mk, mv = _kv_call(mem2d, wmk_b, wmv_b, l)
        xp, q, ta, tb, td, tv = _mix_call(tp, l, xp, zeros(ha, w), zeros(hb, w), zeros(hd, w), wts_p, BF16)
        mk, mv = mk.reshape(1, nbp, -1, LANES), mv.reshape(1, nbp, -1, LANES)
        o = _att_call(q, mk, mv, 0, 1, min(seq, 512), nm, f"att_p_l{l}")
        xp, tf = _ffn_call(tp, l, final, xp, o, zeros(hf, ffn_conv_w.shape[2]), wxo_b, gffn_r, wup_b,
                           ffn_conv_w, fcb_r, wdown_b, gfin_r)
        outs["a_p"].append(_tail_to_state(ta, None, tp, ha))
        outs["b_p"].append(_tail_to_state(tb, None, tp, hb))
        outs["d_p"].append(_tail_to_state(td, None, tp, hd))
        outs["f_p"].append(_tail_to_state(tf, None, tp, hf))
        tv = tv.transpose(0, 2, 1, 3).reshape(tp.nblk, -1, w)
        outs["v_p"].append(_tail_to_state(tv, None, tp, CHUNK))
        outs["mk"].append(_mem_unview(mk[0], nm, X_HEADS))
        outs["mv"].append(_mem_unview(mv[0], nm, X_HEADS))
        sa, sb_, sd, sf = state_conv_a[l], state_conv_b[l], state_pool_d[l], state_ffn_conv[l]
        xs, q, ta, tb, td, tv = _mix_call(ts, l, xs, _state_to_tm(sa, ts), _state_to_tm(sb_, ts),
                                          _state_to_tm(sd, ts), wts_s, F32)
        o = _att_call(q.reshape(nbs, dseq, d), ck, cv, l, 8, dseq, nm, f"att_s_l{l}")
        xs, tf = _ffn_call(ts, l, final, xs, o.reshape(nbs * dseq, d), _state_to_tm(sf, ts), wxo_b, gffn_r,
                           wup_b, ffn_conv_w, fcb_r, wdown_b, gfin_r)
        outs["a_s"].append(_tail_to_state(ta, sa, ts, ha))
        outs["b_s"].append(_tail_to_state(tb, sb_, ts, hb))
        outs["d_s"].append(_tail_to_state(td, sd, ts, hd))
        outs["f_s"].append(_tail_to_state(tf, sf, ts, hf))
        outs["v_s"].append(_tail_to_state(tv, None, ts, dseq))

    st = lambda k: jnp.stack(outs[k])
    return (xp, xs.reshape(nbs, dseq, d),
            st("a_p"), st("b_p"), st("d_p"), st("f_p"), st("v_p"), st("mk"), st("mv"),
            st("a_s"), st("b_s"), st("d_s"), st("f_s"), st("v_s"))
```

```python
import functools
from typing import NamedTuple

import jax
import jax.numpy as jnp
from jax import lax
from jax.experimental import pallas as pl
from jax.experimental.pallas import tpu as pltpu

F32 = jnp.float32
BF16 = jnp.bfloat16

EPS = 1e-6
LANES = 128
SUBLANES = 8
MXU_COLS = 256
CHUNK = 128
POOL_WINDOWS = (2, 4, 8, 16)
N_GROUPS = 4
X_HEADS = 4
PAST_LEN = 16384
VMEM_LIMIT_BYTES = 56 * 1024 * 1024


class Tile(NamedTuple):
    nb: int
    tt: int
    nblk: int
    ntile: int
    past: int
    seq_len: int


def _rms(x, g):
    ms = jnp.mean(x * x, axis=-1, keepdims=True)
    return x * lax.rsqrt(ms + EPS) * g


def _ln(x, g, b):
    mu = jnp.mean(x, axis=-1, keepdims=True)
    xc = x - mu
    var = jnp.mean(xc * xc, axis=-1, keepdims=True)
    return xc * lax.rsqrt(var + EPS) * g + b


def _dot(a, b):
    return jnp.dot(a, b, preferred_element_type=F32)


def _rows(ref, b, tt):
    if len(ref.shape) == 3:
        return ref[b]
    return ref[b * tt:(b + 1) * tt, :]


def _set_rows(ref, b, tt, val):
    if len(ref.shape) == 3:
        ref[b] = val
    else:
        ref[b * tt:(b + 1) * tt, :] = val


def _to_tm(tm, b, val, t):
    for c in range(val.shape[1] // LANES):
        tm[c, pl.ds(b, t.tt, stride=t.nb), :] = val[:, c * LANES:(c + 1) * LANES]


def _from_tm(tm, b, t):
    return jnp.concatenate([tm[c, pl.ds(b, t.tt, stride=t.nb), :] for c in range(tm.shape[0])], axis=1)


def _tm_all(tm):
    return jnp.concatenate([tm[c] for c in range(tm.shape[0])], axis=1)


def _set_tm(tm, val):
    for c in range(tm.shape[0]):
        tm[c] = val[:, c * LANES:(c + 1) * LANES]


def _mix_kernel(t, x_ref, sa_ref, sb_ref, sd_ref,
                gmix_ref, win_ref, caw_ref, cab_ref, lag_ref, lab_ref, wao_ref,
                cbw_ref, cbb_ref, wbo_ref, lcg_ref, lcb_ref, ws_ref, bs_ref, wco_ref,
                wdg_ref, scd_ref, wdo_ref, wmix_ref, gx_ref, wq_ref,
                xo_ref, q_ref, ta_ref, tb_ref, td_ref, tv_ref,
                tm, tm2, h_s, ext_a, hist_b, hist_d, act_s, m_s, *sbufs):
    nb, tt = t.nb, t.tt
    m = nb * tt
    d = h_s.shape[1]
    w = act_s.shape[1]
    ka, sub8 = caw_ref.shape[0], caw_ref.shape[1]
    kb = cbw_ref.shape[0]
    hpa, hpb, hpd = (ka - 1) * nb, (kb - 1) * nb, hist_d.shape[0]
    gw = w // N_GROUPS
    clen = min(t.seq_len, CHUNK)
    short_chunk = clen == tt and tt < 8
    j = pl.program_id(1)

    @pl.when(j == 0)
    def _():
        ext_a[0:hpa, :] = sa_ref[...]
        hist_b[...] = sb_ref[...]
        hist_d[...] = sd_ref[...]
        if not short_chunk:
            tv_ref[...] = jnp.zeros(tv_ref.shape, F32)

    for b in range(nb):
        _to_tm(tm, b, _rms(_rows(x_ref, b, tt), gmix_ref[...]), t)
    h_s[...] = _tm_all(tm).astype(BF16)

    def seg(lo, hi):
        return _dot(h_s[...], win_ref[:, lo:hi])

    gate0 = 8 * w

    def gate(i):
        return jax.nn.sigmoid(seg(gate0 + i * d, gate0 + (i + 1) * d))

    za = seg(0, 2 * w)
    ext_a[hpa:hpa + m, :] = za[:, :w] * jax.nn.sigmoid(za[:, w:])
    rc = 32
    ptiles = [seg(c0, c0 + MXU_COLS) for c0 in range(2 * w, 8 * w, MXU_COLS)]
    for ci, r0 in enumerate(range(0, m, rc)):
        acc = jnp.broadcast_to(cab_ref[...], (rc // sub8, sub8, w))
        for k in range(ka):
            acc = acc + caw_ref[k] * ext_a[r0 + k * nb:r0 + k * nb + rc, :].reshape(rc // sub8, sub8, w)
        y = _ln(acc.reshape(rc, w), lag_ref[...], lab_ref[...])
        act_s[r0:r0 + rc, :] = (y * jax.nn.sigmoid(y)).astype(BF16)
    zrest = jnp.concatenate(ptiles, axis=1)
    m_s[...] = gate(0) * _dot(act_s[...], wao_ref[...])
    ra = ta_ref.shape[0]
    ta_ref[...] = ext_a[hpa + m - ra:hpa + m, :]
    if t.ntile > 1:
        ext_a[0:hpa, :] = ext_a[m:m + hpa, :]

    zb = zrest[:, :3 * w]
    ext_b = jnp.concatenate([hist_b[...], zb[:, w:2 * w] * zb[:, 2 * w:3 * w]], axis=0)
    cb = cbb_ref[...] + cbw_ref[0:1, :] * ext_b[0:m]
    for k in range(1, kb):
        cb = cb + cbw_ref[k:k + 1, :] * ext_b[k * nb:k * nb + m]
    hist_b[...] = ext_b[m:m + hpb]
    tb_ref[...] = ext_b[m:m + hpb]
    m_s[...] += gate(1) * _dot((zb[:, :w] * cb).astype(BF16), wbo_ref[...])

    zc = jax.nn.gelu(zrest[:, 3 * w:5 * w])
    u = zc[:, :w]
    v = _ln(zc[:, w:], lcg_ref[...], lcb_ref[...])
    if short_chunk:
        s_rows = []
        for ti in range(tt):
            s_t = jnp.broadcast_to(bs_ref[ti:ti + 1, :], (nb, w))
            for si in range(ti + 1):
                s_t = s_t + ws_ref[ti, si:si + 1, :] * v[si * nb:(si + 1) * nb]
            s_rows.append(s_t)
        s_all = jnp.concatenate(s_rows, axis=0)
        tv_ref[...] = v
    else:
        vbuf, (sbuf,) = tv_ref, sbufs
        subs = clen // tt
        sub = j % subs
        row0 = pl.multiple_of(sub * m, m)
        for g in range(N_GROUPS):
            vbuf[g, pl.ds(row0, m), :] = v[:, g * gw:(g + 1) * gw]
        t0 = pl.multiple_of(sub * tt, tt)
        ri = t0 + lax.broadcasted_iota(jnp.int32, (tt, clen), 0)
        ci = lax.broadcasted_iota(jnp.int32, (tt, clen), 1)
        bias = bs_ref[pl.ds(t0, tt), :]
        for g in range(N_GROUPS):
            vg = jnp.concatenate([vbuf[g, pl.ds(b, clen, stride=nb), :] for b in range(nb)], axis=1)
            wsg = jnp.where(ri >= ci, ws_ref[g, pl.ds(t0, tt), :], 0.0).astype(BF16)
            sg = _dot(wsg, vg.astype(BF16)) + bias[:, g:g + 1]
            for b in range(nb):
                sbuf[g, pl.ds(b, tt, stride=nb), :] = sg[:, b * gw:(b + 1) * gw]
        s_all = _tm_all(sbuf)
    m_s[...] += gate(2) * _dot((u * s_all).astype(BF16), wco_ref[...])

    xd = zrest[:, 5 * w:6 * w]
    ext_d = jnp.concatenate([hist_d[...], xd], axis=0)
    pos = t.past + j * tt + lax.broadcasted_iota(jnp.int32, (m, gw), 0) // nb
    pgs = []
    for gi, win in enumerate(POOL_WINDOWS):
        cs = slice(gi * gw, (gi + 1) * gw)
        acc = xd[:, cs]
        for lag in range(1, win):
            acc = acc + ext_d[hpd - lag * nb:hpd - lag * nb + m, cs]
        cnt = jnp.minimum(win, pos + 1).astype(F32)
        pgs.append(_dot((acc / cnt - xd[:, cs]).astype(BF16), wdg_ref[gi]))
    hist_d[...] = ext_d[m:m + hpd]
    pd = jnp.concatenate(pgs, axis=1) * scd_ref[...]
    m_s[...] += gate(3) * _dot(pd.astype(BF16), wdo_ref[...])
    rd = td_ref.shape[0]
    td_ref[...] = ext_d[hpd + m - rd:hpd + m]

    _set_tm(tm, _dot(m_s[...].astype(BF16), wmix_ref[...]))
    for b in range(nb):
        x1 = _rows(x_ref, b, tt) + _from_tm(tm, b, t)
        _set_rows(xo_ref, b, tt, x1)
        _to_tm(tm2, b, _rms(x1, gx_ref[...]), t)
    scale = float(d // X_HEADS) ** -0.5
    _set_tm(tm, _dot(_tm_all(tm2).astype(BF16), wq_ref[...]) * scale)
    for b in range(nb):
        _set_rows(q_ref, b, tt, _from_tm(tm, b, t).astype(q_ref.dtype))


def _const(shape, layer):
    nz = len(shape)
    return pl.BlockSpec((None,) + tuple(shape), lambda i, j: (layer,) + (0,) * nz,
                        pipeline_mode=pl.Buffered(1))


def _mix_call(t, layer, x, sa, sb, sd, wts, q_dtype):
    (gmix, win, caw, cab, lag, lab, wao, cbw, cbb, wbo, lcg, lcb, ws, bs, wco,
     wdg, scd, wdo, wmix, gx, wq) = wts
    nb, tt = t.nb, t.tt
    m = nb * tt
    d = win.shape[1]
    w = wao.shape[1]
    hpa, hpb, hpd = sa.shape[1], sb.shape[1], sd.shape[1]
    ra, rd = min(hpa, m), min(hpd, m)
    clen = min(t.seq_len, CHUNK)
    short_chunk = clen == tt and tt < 8
    gw = w // N_GROUPS
    if x.ndim == 3:
        x_spec = pl.BlockSpec((nb, tt, d), lambda i, j: (i, j, 0))
        x_shape = x.shape
    else:
        x_spec = pl.BlockSpec((m, d), lambda i, j: (i * t.ntile + j, 0))
        x_shape = x.shape

    def state_spec(rows):
        return pl.BlockSpec((None, rows, w), lambda i, j: (i, 0, 0))

    in_specs = [x_spec, state_spec(hpa), state_spec(hpb), state_spec(hpd)]
    in_specs += [_const(a.shape[1:], layer) for a in wts]
    out_shape = [jax.ShapeDtypeStruct(x_shape, F32), jax.ShapeDtypeStruct(x_shape, q_dtype),
                 jax.ShapeDtypeStruct((t.nblk, ra, w), F32), jax.ShapeDtypeStruct((t.nblk, hpb, w), F32),
                 jax.ShapeDtypeStruct((t.nblk, rd, w), F32)]
    out_specs = [x_spec, x_spec, state_spec(ra), state_spec(hpb), state_spec(rd)]
    if short_chunk:
        out_shape.append(jax.ShapeDtypeStruct((t.nblk, m, w), F32))
        out_specs.append(state_spec(m))
    else:
        out_shape.append(jax.ShapeDtypeStruct((t.nblk, N_GROUPS, clen * nb, gw), F32))
        out_specs.append(pl.BlockSpec((None, N_GROUPS, clen * nb, gw), lambda i, j: (i, 0, 0, 0)))
    scratch = [pltpu.VMEM((d // LANES, m, LANES), F32), pltpu.VMEM((d // LANES, m, LANES), F32),
               pltpu.VMEM((m, d), BF16), pltpu.VMEM((hpa + m, w), F32),
               pltpu.VMEM((hpb, w), F32), pltpu.VMEM((hpd, w), F32),
               pltpu.VMEM((m, w), BF16), pltpu.VMEM((m, d), F32)]
    if not short_chunk:
        scratch.append(pltpu.VMEM((N_GROUPS, m, gw), F32))
    return pl.pallas_call(
        functools.partial(_mix_kernel, t),
        grid=(t.nblk, t.ntile),
        in_specs=in_specs, out_specs=out_specs, out_shape=out_shape,
        scratch_shapes=scratch,
        compiler_params=pltpu.CompilerParams(
            dimension_semantics=("arbitrary", "arbitrary"), vmem_limit_bytes=VMEM_LIMIT_BYTES),
        name=f"mix_l{layer}_nb{nb}",
    )(x, sa, sb, sd, *wts)


def _head_rows(ref, b, h, nm):
    per_m = ref.shape[1] // nm
    nlb = per_m // X_HEADS
    return jnp.concatenate(
        [ref[b, pl.ds(lb * X_HEADS + h, nm, stride=per_m), :] for lb in range(nlb)], axis=1)


def _att_kernel(nm, q_ref, k_ref, v_ref, o_ref, s_scr):
    sb, tq, d = q_ref.shape
    hd = d // X_HEADS
    for b in range(sb):
        q = q_ref[b].astype(BF16)
        for h in range(X_HEADS):
            kh = _head_rows(k_ref, b, h, nm).astype(BF16)
            r0 = (b * X_HEADS + h) * tq
            s_scr[r0:r0 + tq, :] = lax.dot_general(q[:, h * hd:(h + 1) * hd], kh, (((1,), (1,)), ((), ())),
                                                   preferred_element_type=F32)
    s = s_scr[...]
    e = jnp.exp(s - jnp.max(s, axis=-1, keepdims=True))
    s_scr[...] = e / jnp.sum(e, axis=-1, keepdims=True)
    for b in range(sb):
        outs = []
        for h in range(X_HEADS):
            vh = _head_rows(v_ref, b, h, nm).astype(BF16)
            r0 = (b * X_HEADS + h) * tq
            outs.append(_dot(s_scr[r0:r0 + tq, :].astype(BF16), vh))
        o_ref[b] = jnp.concatenate(outs, axis=1).astype(o_ref.dtype)


def _att_call(q, k, v, layer, sb, tq, nm, name):
    n, tlen, d = q.shape
    rows = k.shape[2]
    q_spec = pl.BlockSpec((sb, tq, d), lambda i, j: (i, j, 0))
    kv_spec = pl.BlockSpec((None, sb, rows, LANES), lambda i, j: (layer, i, 0, 0))
    return pl.pallas_call(
        functools.partial(_att_kernel, nm),
        grid=(n // sb, tlen // tq),
        in_specs=[q_spec, kv_spec, kv_spec], out_specs=q_spec,
        out_shape=jax.ShapeDtypeStruct(q.shape, q.dtype),
        scratch_shapes=[pltpu.VMEM((sb * X_HEADS * tq, nm), F32)],
        compiler_params=pltpu.CompilerParams(
            dimension_semantics=("arbitrary", "arbitrary"), vmem_limit_bytes=VMEM_LIMIT_BYTES),
        name=name,
    )(q, k, v)


def _ffn_kernel(t, final, x_ref, o_ref, sf_ref, wxo_ref, gffn_ref, wup_ref, fcw_ref, fcb_ref,
                wdown_ref, gfin_ref, xo_ref, tf_ref, tm, h_s, hist_f, act_s):
    nb, tt = t.nb, t.tt
    m = nb * tt
    d = h_s.shape[1]
    dff = act_s.shape[1]
    kf = fcw_ref.shape[0]
    hp = (kf - 1) * nb
    j = pl.program_id(1)

    @pl.when(j == 0)
    def _():
        hist_f[...] = sf_ref[...]

    o2 = o_ref[...]
    if len(o2.shape) == 3:
        o2 = o2.reshape(m, d)
    ao = _dot(o2.astype(BF16), wxo_ref[...])
    for b in range(nb):
        x2 = _rows(x_ref, b, tt) + ao[b * tt:(b + 1) * tt]
        _set_rows(xo_ref, b, tt, x2)
        _to_tm(tm, b, _rms(x2, gffn_ref[...]), t)
    h_s[...] = _tm_all(tm).astype(BF16)

    cw = 256
    for c in range(dff // cw):
        parts = []
        for col0 in (c * cw, dff + c * cw):
            cs = slice(col0, col0 + cw)
            up = _dot(h_s[...], wup_ref[:, cs])
            ext = jnp.concatenate([hist_f[:, cs], up], axis=0)
            cv = fcb_ref[:, cs] + fcw_ref[0:1, cs] * ext[0:m]
            for k in range(1, kf):
                cv = cv + fcw_ref[k:k + 1, cs] * ext[k * nb:k * nb + m]
            hist_f[:, cs] = ext[m:m + hp]
            parts.append(cv)
        act_s[:, c * cw:(c + 1) * cw] = (parts[0] * jax.nn.sigmoid(parts[0]) * parts[1]).astype(BF16)

    tf_ref[...] = hist_f[...]
    _set_tm(tm, _dot(act_s[...], wdown_ref[...]))
    for b in range(nb):
        x3 = _rows(xo_ref, b, tt) + _from_tm(tm, b, t)
        if final:
            x3 = _rms(x3, gfin_ref[...])
        _set_rows(xo_ref, b, tt, x3)


def _ffn_call(t, layer, final, x, o, sf, wxo, gffn, wup, fcw, fcb, wdown, gfin):
    nb, tt = t.nb, t.tt
    m = nb * tt
    d = wxo.shape[1]
    dff = wdown.shape[1]
    hp = sf.shape[1]
    if x.ndim == 3:
        x_spec = pl.BlockSpec((nb, tt, d), lambda i, j: (i, j, 0))
    else:
        x_spec = pl.BlockSpec((m, d), lambda i, j: (i * t.ntile + j, 0))
    sf_spec = pl.BlockSpec((None, hp, 2 * dff), lambda i, j: (i, 0, 0))
    wts = (wxo, gffn, wup, fcw, fcb, wdown)
    in_specs = [x_spec, x_spec, sf_spec] + [_const(a.shape[1:], layer) for a in wts]
    in_specs.append(pl.BlockSpec(gfin.shape, lambda i, j: (0, 0), pipeline_mode=pl.Buffered(1)))
    return pl.pallas_call(
        functools.partial(_ffn_kernel, t, final),
        grid=(t.nblk, t.ntile),
        in_specs=in_specs, out_specs=[x_spec, sf_spec],
        out_shape=[jax.ShapeDtypeStruct(x.shape, F32), jax.ShapeDtypeStruct((t.nblk, hp, 2 * dff), F32)],
        scratch_shapes=[pltpu.VMEM((d // LANES, m, LANES), F32), pltpu.VMEM((m, d), BF16),
                        pltpu.VMEM((hp, 2 * dff), F32), pltpu.VMEM((m, dff), BF16)],
        compiler_params=pltpu.CompilerParams(
            dimension_semantics=("arbitrary", "arbitrary"), vmem_limit_bytes=VMEM_LIMIT_BYTES),
        name=f"ffn_l{layer}_nb{nb}",
    )(x, o, sf, *wts, gfin)


def _kv_kernel(x_ref, wk_ref, wv_ref, k_ref, v_ref):
    x = x_ref[...].astype(BF16)
    bm, d = x.shape
    hd = d // X_HEADS
    nlb = hd // LANES
    per_m = nlb * X_HEADS
    for w_ref, out_ref in ((wk_ref, k_ref), (wv_ref, v_ref)):
        y = _dot(x, w_ref[...])
        for h in range(X_HEADS):
            for lb in range(nlb):
                c0 = h * hd + lb * LANES
                out_ref[pl.ds(lb * X_HEADS + h, bm, stride=per_m), :] = y[:, c0:c0 + LANES]


def _kv_call(mem2d, wk, wv):
    rows, d = mem2d.shape
    depth = wk.shape[0]
    bm = 512
    per_m = d // LANES
    r_spec = pl.BlockSpec((bm, d), lambda l, i: (i, 0))
    o_spec = pl.BlockSpec((None, bm * per_m, LANES), lambda l, i: (l, i, 0))
    w_spec = pl.BlockSpec((None, d, d), lambda l, i: (l, 0, 0))
    return pl.pallas_call(
        _kv_kernel, grid=(depth, rows // bm),
        in_specs=[r_spec, w_spec, w_spec], out_specs=[o_spec, o_spec],
        out_shape=[jax.ShapeDtypeStruct((depth, rows * per_m, LANES), F32)] * 2,
        compiler_params=pltpu.CompilerParams(dimension_semantics=("arbitrary", "arbitrary")),
        name="kv",
    )(mem2d, wk, wv)


def _state_to_tm(state, t):
    n, h, c = state.shape
    return state.reshape(t.nblk, t.nb, h, c).transpose(0, 2, 1, 3).reshape(t.nblk, h * t.nb, c)


def _mem_view(c):
    *lead, nm, h, hd = c.shape
    nlb = hd // LANES
    n = len(lead)
    c = c.reshape(*lead, nm, h, nlb, LANES)
    c = c.transpose(*range(n), n, n + 2, n + 1, n + 3)
    return c.reshape(*lead, nm * nlb * h, LANES)


def _mem_unview(g, nm, h):
    *lead, rows, _ = g.shape
    nlb = rows // (nm * h)
    n = len(lead)
    g = g.reshape(*lead, nm, nlb, h, LANES)
    g = g.transpose(*range(n), n, n + 2, n + 1, n + 3)
    return g.reshape(*lead, nm, h, nlb * LANES)


def _tails_to_state(tails, old, t, h):
    tail = jnp.stack(tails)
    depth, nblk, r, c = tail.shape
    k = r // t.nb
    new = tail.reshape(depth, nblk, k, t.nb, c).transpose(0, 1, 3, 2, 4).reshape(depth, nblk * t.nb, k, c)
    if k < h:
        new = jnp.concatenate([old[:, :, k:], new], axis=2)
    return new


def kernel(x_prompt, x_sample, mem_prompt, state_conv_a, state_conv_b, state_pool_d, state_ffn_conv,
           cache_mem_k, cache_mem_v, g_mix, w_in, conv_a_w, conv_a_b, ln_a_g, ln_a_b, w_a_out,
           conv_b_w, conv_b_b, w_b_out, ln_c_g, ln_c_b, w_s, b_s, w_c_out, w_d_grp, scale_d, w_d_out,
           w_mix_out, g_xattn, w_q, w_mk, w_mv, w_xo, g_ffn, w_up, ffn_conv_w, ffn_conv_b, w_down, g_final):
    depth = w_in.shape[0]
    nbp, seq, d = x_prompt.shape
    nbs, dseq, _ = x_sample.shape
    w = w_a_out.shape[1]
    nm = mem_prompt.shape[1]
    ha, hb, hd, hf = conv_a_w.shape[1] - 1, conv_b_w.shape[1] - 1, max(POOL_WINDOWS) - 1, ffn_conv_w.shape[1] - 1

    tp = Tile(nb=nbp, tt=64, nblk=1, ntile=seq // 64, past=0, seq_len=seq)
    ts = Tile(nb=32, tt=dseq, nblk=nbs // 32, ntile=1, past=PAST_LEN, seq_len=dseq)

    row = lambda a: a.reshape(depth, 1, a.shape[-1])
    bf = lambda a: a.astype(BF16)
    common = dict(
        gmix=row(g_mix), win=bf(w_in), caw=jnp.broadcast_to(conv_a_w[:, :, None, :], conv_a_w.shape[:2] + (SUBLANES, w)),
        cab=row(conv_a_b), lag=row(ln_a_g), lab=row(ln_a_b),
        wao=bf(w_a_out), cbw=conv_b_w, cbb=row(conv_b_b), wbo=bf(w_b_out), lcg=row(ln_c_g), lcb=row(ln_c_b),
        wco=bf(w_c_out), wdg=bf(w_d_grp), scd=row(scale_d), wdo=bf(w_d_out), wmix=bf(w_mix_out),
        gx=row(g_xattn), wq=bf(w_q))
    order = ("gmix", "win", "caw", "cab", "lag", "lab", "wao", "cbw", "cbb", "wbo", "lcg", "lcb",
             "ws", "bs", "wco", "wdg", "scd", "wdo", "wmix", "gx", "wq")
    wts_p = dict(common, ws=w_s, bs=b_s.transpose(0, 2, 1))
    gw = w // N_GROUPS
    ws_x = jnp.repeat(w_s[:, :, :dseq, :dseq].transpose(0, 2, 3, 1), gw, axis=-1)
    bs_x = jnp.repeat(b_s[:, :, :dseq].transpose(0, 2, 1), gw, axis=-1)
    wts_s = dict(common, ws=ws_x, bs=bs_x)
    wts_p = tuple(wts_p[k] for k in order)
    wts_s = tuple(wts_s[k] for k in order)

    wxo_b, wup_b, wdown_b, wmk_b, wmv_b = bf(w_xo), bf(w_up), bf(w_down), bf(w_mk), bf(w_mv)
    gffn_r, fcb_r, gfin_r = row(g_ffn), row(ffn_conv_b), g_final.reshape(1, d)

    xp = x_prompt
    xs = x_sample.reshape(nbs * dseq, d)
    mem2d = mem_prompt.reshape(nbp * nm, d)
    ck, cv = _mem_view(cache_mem_k), _mem_view(cache_mem_v)
    zeros = lambda h, c: jnp.zeros((1, h * nbp, c), F32)

    mk, mv = _kv_call(mem2d, wmk_b, wmv_b)
    mk, mv = mk.reshape(depth, nbp, -1, LANES), mv.reshape(depth, nbp, -1, LANES)
    outs = {k: [] for k in ("a_p", "b_p", "d_p", "f_p", "v_p", "a_s", "b_s", "d_s", "f_s", "v_s")}
    for l in range(depth):
        final = l == depth - 1
        xp, q, ta, tb, td, tv = _mix_call(tp, l, xp, zeros(ha, w), zeros(hb, w), zeros(hd, w), wts_p, BF16)
        o = _att_call(q, mk, mv, l, 1, min(seq, 512), nm, f"att_p_l{l}")
        xp, tf = _ffn_call(tp, l, final, xp, o, zeros(hf, ffn_conv_w.shape[2]), wxo_b, gffn_r, wup_b,
                           ffn_conv_w, fcb_r, wdown_b, gfin_r)
        tv = tv.transpose(0, 2, 1, 3).reshape(tp.nblk, -1, w)
        for k, tail in zip(("a_p", "b_p", "d_p", "f_p", "v_p"), (ta, tb, td, tf, tv)):
            outs[k].append(tail)
        sa, sb_, sd, sf = state_conv_a[l], state_conv_b[l], state_pool_d[l], state_ffn_conv[l]
        xs, q, ta, tb, td, tv = _mix_call(ts, l, xs, _state_to_tm(sa, ts), _state_to_tm(sb_, ts),
                                          _state_to_tm(sd, ts), wts_s, F32)
        o = _att_call(q.reshape(nbs, dseq, d), ck, cv, l, 8, dseq, nm, f"att_s_l{l}")
        xs, tf = _ffn_call(ts, l, final, xs, o.reshape(nbs * dseq, d), _state_to_tm(sf, ts), wxo_b, gffn_r,
                           wup_b, ffn_conv_w, fcb_r, wdown_b, gfin_r)
        for k, tail in zip(("a_s", "b_s", "d_s", "f_s", "v_s"), (ta, tb, td, tf, tv)):
            outs[k].append(tail)

    return (xp, xs.reshape(nbs, dseq, d),
            _tails_to_state(outs["a_p"], None, tp, ha), _tails_to_state(outs["b_p"], None, tp, hb),
            _tails_to_state(outs["d_p"], None, tp, hd), _tails_to_state(outs["f_p"], None, tp, hf),
            _tails_to_state(outs["v_p"], None, tp, CHUNK),
            _mem_unview(mk, nm, X_HEADS), _mem_unview(mv, nm, X_HEADS),
            _tails_to_state(outs["a_s"], state_conv_a, ts, ha), _tails_to_state(outs["b_s"], state_conv_b, ts, hb),
            _tails_to_state(outs["d_s"], state_pool_d, ts, hd), _tails_to_state(outs["f_s"], state_ffn_conv, ts, hf),
            _tails_to_state(outs["v_s"], None, ts, dseq))
```
